```python
import math
import jax, jax.numpy as jnp
from jax import lax
import numpy as np

D_MODEL = 1024
BATCH = 2
SEQ = 8192
DEPTH = 2

CHUNK = 64
Q_BLOCK = 128
D_MIX = D_MODEL
M_WIDTH = D_MIX // 2
M_HEADS = 4
M_HEAD_DIM = M_WIDTH // M_HEADS
CONV_WIDTH = 4
A_WIDTH = D_MIX - M_WIDTH
A_HEADS = 4
A_VDIM = A_WIDTH // A_HEADS
A_QK_DIM = A_VDIM // 2
ROPE_THETA = 10000.0
EPS = 1e-6
PROJ_SIZES = (M_WIDTH, M_WIDTH, M_WIDTH, M_HEADS, M_HEADS, M_WIDTH,
              A_WIDTH, A_WIDTH, A_WIDTH, A_WIDTH)
D_IN = 4 * M_WIDTH + 2 * M_HEADS + 4 * A_WIDTH

kernel_name = "hymba_mlstm_diffattn_trunk"


def rmsnorm(x, w):
    xf = x.astype(jnp.float32)
    y = xf * lax.rsqrt(jnp.mean(xf * xf, axis=-1, keepdims=True) + EPS)
    return (y * w.astype(jnp.float32)).astype(x.dtype)


def causal_conv(x, w, b):
    K, C = w.shape
    y = lax.conv_general_dilated(
        x, w.reshape(K, 1, C).astype(x.dtype), window_strides=(1,),
        padding=[(K - 1, 0)], dimension_numbers=('NWC', 'WIO', 'NWC'),
        feature_group_count=C)
    return y + b


def rope(x):
    S, dh = x.shape[1], x.shape[-1]
    inv = 1.0 / (ROPE_THETA ** (jnp.arange(0, dh, 2, dtype=jnp.float32) / dh))
    ang = jnp.arange(S, dtype=jnp.float32)[:, None] * inv[None, :]
    cos = jnp.concatenate([jnp.cos(ang), jnp.cos(ang)], -1)[None, :, None, None, :]
    sin = jnp.concatenate([jnp.sin(ang), jnp.sin(ang)], -1)[None, :, None, None, :]
    xf = x.astype(jnp.float32)
    x1, x2 = jnp.split(xf, 2, axis=-1)
    rot = jnp.concatenate([-x2, x1], -1)
    return (xf * cos + rot * sin).astype(x.dtype)


def mlstm_chunkwise(q, k, v, i_pre, f_pre):
    B, S, H, D = q.shape
    L = CHUNK
    N = S // L

    def to_chunks(t):
        return t.astype(jnp.float32).reshape(B, N, L, H, -1).transpose(1, 0, 3, 2, 4)

    qc = to_chunks(q)
    kc = to_chunks(k) * (D ** -0.5)
    vc = to_chunks(v)
    lf = jax.nn.log_sigmoid(f_pre.astype(jnp.float32)).reshape(B, N, L, H).transpose(1, 0, 3, 2)
    li = i_pre.astype(jnp.float32).reshape(B, N, L, H).transpose(1, 0, 3, 2)
    causal = jnp.tril(jnp.ones((L, L), dtype=bool))

    def step(carry, inp):
        C, n, m = carry
        q_, k_, v_, lf_, li_ = inp
        b = jnp.cumsum(lf_, axis=-1)
        dmat = jnp.where(causal, b[..., :, None] - b[..., None, :] + li_[..., None, :], -jnp.inf)
        inter = b + m[..., None]
        m_i = jnp.maximum(inter, jnp.max(dmat, axis=-1))
        w_intra = jnp.exp(dmat - m_i[..., None])
        w_inter = jnp.exp(inter - m_i)
        s = jnp.einsum('bhid,bhjd->bhij', q_, k_) * w_intra
        num = (w_inter[..., None] * jnp.einsum('bhid,bhde->bhie', q_, C)
               + jnp.einsum('bhij,bhje->bhie', s, v_))
        den = w_inter * jnp.einsum('bhid,bhd->bhi', q_, n) + jnp.sum(s, axis=-1)
        h = num / jnp.maximum(jnp.abs(den), jnp.exp(-m_i))[..., None]
        b_last = b[..., -1]
        g = b_last[..., None] - b + li_
        m_new = jnp.maximum(b_last + m, jnp.max(g, axis=-1))
        decay = jnp.exp(b_last + m - m_new)
        wk = jnp.exp(g - m_new[..., None])
        C_new = decay[..., None, None] * C + jnp.einsum('bhl,bhld,bhle->bhde', wk, k_, v_)
        n_new = decay[..., None] * n + jnp.einsum('bhl,bhld->bhd', wk, k_)
        return (C_new, n_new, m_new), h

    init = (jnp.zeros((B, H, D, D), jnp.float32), jnp.zeros((B, H, D), jnp.float32),
            jnp.zeros((B, H), jnp.float32))
    _, hs = lax.scan(step, init, (qc, kc, vc, lf, li))
    return hs.transpose(1, 0, 3, 2, 4).reshape(B, S, H, D).astype(q.dtype)


def diff_attention(q, k, v, lam):
    B, S, H, _, dk = q.shape
    dv = v.shape[-1]
    nb = S // Q_BLOCK
    qb = q.astype(jnp.float32).reshape(B, nb, Q_BLOCK, H, 2, dk).transpose(1, 0, 3, 4, 2, 5)
    kt = k.astype(jnp.float32).transpose(0, 2, 3, 1, 4)
    vt = v.astype(jnp.float32).transpose(0, 2, 1, 3)
    key_chunk = jnp.arange(S) // CHUNK
    scale = dk ** -0.5

    def block(args):
        qblk, start = args
        s = jnp.einsum('bhcqd,bhckd->bhcqk', qblk, kt) * scale
        q_chunk = (start + jnp.arange(Q_BLOCK)) // CHUNK
        mask = key_chunk[None, :] <= q_chunk[:, None]
        p = jax.nn.softmax(jnp.where(mask, s, -jnp.inf), axis=-1)
        a = p[:, :, 0] - lam * p[:, :, 1]
        return jnp.einsum('bhqk,bhke->bhqe', a, vt)

    out = lax.map(block, (qb, jnp.arange(nb) * Q_BLOCK))
    return out.transpose(1, 0, 3, 2, 4).reshape(B, S, H, dv).astype(v.dtype)


def hybrid_layer(x, norm_w, w_in, conv_w, conv_b, i_bias, f_bias, m_norm_w, m_skip,
                 lam_q1, lam_k1, lam_q2, lam_k2, a_norm_w, w_out, lambda_init):
    B, S, _ = x.shape
    h = rmsnorm(x, norm_w)
    proj = jnp.einsum('bsd,de->bse', h, w_in)
    idx = list(np.cumsum(PROJ_SIZES)[:-1])
    mq, mk, mv, mi, mf, mz, aq, ak, av, az = jnp.split(proj, idx, axis=-1)

    qk = jax.nn.silu(causal_conv(jnp.concatenate([mq, mk], -1), conv_w, conv_b))
    mq_c, mk_c = jnp.split(qk, 2, axis=-1)
    to_heads = lambda t: t.reshape(B, S, M_HEADS, M_HEAD_DIM)
    hm = mlstm_chunkwise(to_heads(mq_c), to_heads(mk_c), to_heads(mv), mi + i_bias, mf + f_bias)
    hm = rmsnorm(hm, m_norm_w.reshape(M_HEADS, M_HEAD_DIM)).reshape(B, S, M_WIDTH)
    ym = (hm + m_skip * mq_c) * jax.nn.silu(mz)

    qa = rope(aq.reshape(B, S, A_HEADS, 2, A_QK_DIM))
    ka = rope(ak.reshape(B, S, A_HEADS, 2, A_QK_DIM))
    lam = (jnp.exp(jnp.sum(lam_q1.astype(jnp.float32) * lam_k1.astype(jnp.float32)))
           - jnp.exp(jnp.sum(lam_q2.astype(jnp.float32) * lam_k2.astype(jnp.float32)))
           + lambda_init)
    ha = diff_attention(qa, ka, av.reshape(B, S, A_HEADS, A_VDIM), lam)
    ha = (rmsnorm(ha, a_norm_w) * (1.0 - lambda_init)).reshape(B, S, A_WIDTH)
    ya = ha * jax.nn.silu(az)

    y = jnp.einsum('bse,ed->bsd', jnp.concatenate([ym, ya], -1), w_out)
    return x + y


def setup_inputs(seed: int = 0) -> dict:
    key = jax.random.key(seed)
    ks = jax.random.split(key, 16)
    f32 = jnp.float32
    nrm = lambda k, shape, s: jax.random.normal(k, shape, f32) * s
    f_base = jnp.linspace(3.0, 6.0, M_HEADS, dtype=f32)[None, :]
    return {
        "x": nrm(ks[0], (BATCH, SEQ, D_MODEL), 1.0),
        "norm_w": 1.0 + nrm(ks[1], (DEPTH, D_MODEL), 0.02),
        "w_in": nrm(ks[2], (DEPTH, D_MODEL, D_IN), D_MODEL ** -0.5),
        "conv_w": nrm(ks[3], (DEPTH, CONV_WIDTH, 2 * M_WIDTH), CONV_WIDTH ** -0.5),
        "conv_b": nrm(ks[4], (DEPTH, 2 * M_WIDTH), 0.01),
        "i_bias": nrm(ks[5], (DEPTH, M_HEADS), 0.1),
        "f_bias": f_base + nrm(ks[6], (DEPTH, M_HEADS), 0.1),
        "m_norm_w": 1.0 + nrm(ks[7], (DEPTH, M_WIDTH), 0.02),
        "m_skip": 1.0 + nrm(ks[8], (DEPTH, M_WIDTH), 0.02),
        "lam_q1": nrm(ks[9], (DEPTH, A_QK_DIM), 0.1),
        "lam_k1": nrm(ks[10], (DEPTH, A_QK_DIM), 0.1),
        "lam_q2": nrm(ks[11], (DEPTH, A_QK_DIM), 0.1),
        "lam_k2": nrm(ks[12], (DEPTH, A_QK_DIM), 0.1),
        "a_norm_w": 1.0 + nrm(ks[13], (DEPTH, A_VDIM), 0.02),
        "w_out": nrm(ks[14], (DEPTH, D_MIX, D_MODEL), D_MIX ** -0.5),
        "final_norm_w": 1.0 + nrm(ks[15], (D_MODEL,), 0.02),
    }


def reference(x, norm_w, w_in, conv_w, conv_b, i_bias, f_bias, m_norm_w, m_skip,
              lam_q1, lam_k1, lam_q2, lam_k2, a_norm_w, w_out, final_norm_w):
    for l in range(DEPTH):
        lambda_init = 0.8 - 0.6 * math.exp(-0.3 * l)
        x = hybrid_layer(x, norm_w[l], w_in[l], conv_w[l], conv_b[l], i_bias[l], f_bias[l],
                         m_norm_w[l], m_skip[l], lam_q1[l], lam_k1[l], lam_q2[l], lam_k2[l],
                         a_norm_w[l], w_out[l], lambda_init)
    return rmsnorm(x, final_norm_w)
```

```python
import functools
import math

import jax
import jax.numpy as jnp
from jax import lax
from jax.experimental import pallas as pl
from jax.experimental.pallas import tpu as pltpu

F32 = jnp.float32
BF16 = jnp.bfloat16

D_MODEL = 1024
M_WIDTH = 512
M_HEADS = 4
M_HEAD_DIM = 128
CONV_WIDTH = 4
A_WIDTH = 512
A_HEADS = 4
A_VDIM = 128
A_QK_DIM = 64
ATTN_CHUNK = 64
ROPE_THETA = 10000.0
EPS = 1e-6
LOG2E = 1.4426950408889634

P_WIDTH = 4096
GATE_LANES = 128

VMEM_LIMIT = 56 * 1024 * 1024
NEG_BIG = -1e30


def _silu(y):
    return y * (1.0 / (1.0 + jnp.exp(-y)))


def _inproj_kernel(x_ref, nw_ref, w_ref, wg_ref, p_ref, g_ref, h_scr):
    @pl.when(pl.program_id(1) == 0)
    def _():
        x = x_ref[...]
        ms = jnp.mean(x * x, axis=-1, keepdims=True)
        h = (x * lax.rsqrt(ms + EPS) * nw_ref[...]).astype(BF16)
        h_scr[...] = h
        g_ref[...] = jnp.dot(h, wg_ref[...], preferred_element_type=F32)

    p_ref[...] = jnp.dot(h_scr[...], w_ref[...], preferred_element_type=F32).astype(BF16)


def _inproj(x2, norm_w, w_main, w_gate, *, tm=1024, tn=1024):
    rows = x2.shape[0]
    return pl.pallas_call(
        _inproj_kernel,
        grid=(rows // tm, P_WIDTH // tn),
        in_specs=[
            pl.BlockSpec((tm, D_MODEL), lambda i, j: (i, 0)),
            pl.BlockSpec((1, D_MODEL), lambda i, j: (0, 0)),
            pl.BlockSpec((D_MODEL, tn), lambda i, j: (0, j)),
            pl.BlockSpec((D_MODEL, GATE_LANES), lambda i, j: (0, 0)),
        ],
        out_specs=[
            pl.BlockSpec((tm, tn), lambda i, j: (i, j)),
            pl.BlockSpec((tm, GATE_LANES), lambda i, j: (i, 0)),
        ],
        out_shape=[
            jax.ShapeDtypeStruct((rows, P_WIDTH), BF16),
            jax.ShapeDtypeStruct((rows, GATE_LANES), F32),
        ],
        scratch_shapes=[pltpu.VMEM((tm, D_MODEL), BF16)],
        compiler_params=pltpu.CompilerParams(
            dimension_semantics=("arbitrary", "arbitrary"),
            vmem_limit_bytes=VMEM_LIMIT),
        name="inproj",
    )(x2, norm_w.reshape(1, D_MODEL), w_main, w_gate)


def _dot_f32_exact_lhs01(tri, x):
    hi = x.astype(BF16)
    r1 = x - hi.astype(F32)
    mid = r1.astype(BF16)
    lo = (r1 - mid.astype(F32)).astype(BF16)
    d = lambda t: jnp.dot(tri, t, preferred_element_type=F32)
    return d(hi) + d(mid) + d(lo)


def _mlstm_kernel(qk_ref, v_ref, z_ref, g_ref, cw_ref, cb_ref, gb_ref, nw_ref, sk_ref,
                  o_ref, xbuf, c_scr, m_scr, *, L):
    n = pl.program_id(1)
    D = M_HEAD_DIM

    @pl.when(n == 0)
    def _():
        xbuf[0:8, :] = jnp.zeros((8, 2 * M_WIDTH), F32)
        c_scr[...] = jnp.zeros(c_scr.shape, F32)
        m_scr[...] = jnp.zeros(m_scr.shape, F32)

    xbuf[8:8 + L, :] = qk_ref[...].astype(F32)
    y = cb_ref[...] + cw_ref[3:4, :] * xbuf[8:8 + L, :]
    y = y + cw_ref[2:3, :] * xbuf[7:7 + L, :]
    y = y + cw_ref[1:2, :] * xbuf[6:6 + L, :]
    y = y + cw_ref[0:1, :] * xbuf[5:5 + L, :]
    xbuf[0:8, :] = xbuf[L:L + 8, :]
    qkc = _silu(y)

    g = g_ref[...] + gb_ref[...]
    lf = jnp.minimum(g, 0.0) - jnp.log1p(jnp.exp(-jnp.abs(g)))
    row = lax.broadcasted_iota(jnp.int32, (L, L), 0)
    col = lax.broadcasted_iota(jnp.int32, (L, L), 1)
    causal = col <= row
    tri = jnp.where(causal, 1.0, 0.0).astype(BF16)
    bcum = _dot_f32_exact_lhs01(tri, lf)
    lane = lax.broadcasted_iota(jnp.int32, (L, GATE_LANES), 1)
    a_col = jnp.where(lane < M_HEADS, g, bcum)
    a_row = a_col.T

    ones = jnp.ones((L, D), BF16)
    for h in range(M_HEADS):
        sl = slice(h * D, (h + 1) * D)
        b_c = a_col[:, M_HEADS + h:M_HEADS + h + 1]
        li_c = a_col[:, h:h + 1]
        b_r = a_row[M_HEADS + h:M_HEADS + h + 1, :]
        li_r = a_row[h:h + 1, :]
        m_prev = m_scr[h:h + 1, 0:1]

        dm = jnp.where(causal, b_c - b_r + li_r, NEG_BIG)
        inter = b_c + m_prev
        m_i = jnp.maximum(inter, jnp.max(dm, axis=1, keepdims=True))
        w_intra = jnp.exp(dm - m_i)
        w_inter = jnp.exp(inter - m_i)

        qh = qkc[:, sl]
        kh = qkc[:, M_WIDTH + h * D:M_WIDTH + (h + 1) * D] * (D ** -0.5)
        qb = qh.astype(BF16)
        v_ext = jnp.concatenate([v_ref[:, sl], ones], axis=1)
        s = lax.dot_general(qb, kh.astype(BF16), (((1,), (1,)), ((), ())),
                            preferred_element_type=F32) * w_intra
        c_old = c_scr[h]
        q_c = jnp.dot(qb, c_old.astype(BF16), preferred_element_type=F32)
        s_v = jnp.dot(s.astype(BF16), v_ext, preferred_element_type=F32)
        num = w_inter * q_c[:, :D] + s_v[:, :D]
        den = w_inter * q_c[:, D:D + 1] + s_v[:, D:D + 1]
        hm = num / jnp.maximum(jnp.abs(den), jnp.exp(-m_i))

        b_last = b_c[L - 1:L, :]
        gg = b_last - b_c + li_c
        m_new = jnp.maximum(b_last + m_prev, jnp.max(gg, axis=0, keepdims=True))
        decay = jnp.exp(b_last + m_prev - m_new)
        wk = jnp.exp(gg - m_new)
        kw_t = (kh * wk).T.astype(BF16)
        c_scr[h] = decay * c_old + jnp.dot(kw_t, v_ext, preferred_element_type=F32)
        m_scr[h:h + 1, :] = jnp.broadcast_to(m_new, (1, m_scr.shape[1]))

        ms = jnp.mean(hm * hm, axis=-1, keepdims=True)
        hn = hm * lax.rsqrt(ms + EPS) * nw_ref[:, sl]
        ym = (hn + sk_ref[:, sl] * qh) * _silu(z_ref[:, sl].astype(F32))
        o_ref[:, sl] = ym.astype(BF16)


def _mlstm(p, gates, conv_w, conv_b, gate_bias, m_norm_w, m_skip, *, batch, seq, L=256):
    nchunk = seq // L
    rb = lambda b, n: b * nchunk + n
    return pl.pallas_call(
        functools.partial(_mlstm_kernel, L=L),
        grid=(batch, nchunk),
        in_specs=[
            pl.BlockSpec((L, 2 * M_WIDTH), lambda b, n: (rb(b, n), 0)),
            pl.BlockSpec((L, M_WIDTH), lambda b, n: (rb(b, n), 2)),
            pl.BlockSpec((L, M_WIDTH), lambda b, n: (rb(b, n), 3)),
            pl.BlockSpec((L, GATE_LANES), lambda b, n: (rb(b, n), 0)),
            pl.BlockSpec((CONV_WIDTH, 2 * M_WIDTH), lambda b, n: (0, 0)),
            pl.BlockSpec((1, 2 * M_WIDTH), lambda b, n: (0, 0)),
            pl.BlockSpec((1, GATE_LANES), lambda b, n: (0, 0)),
            pl.BlockSpec((1, M_WIDTH), lambda b, n: (0, 0)),
            pl.BlockSpec((1, M_WIDTH), lambda b, n: (0, 0)),
        ],
        out_specs=pl.BlockSpec((L, M_WIDTH), lambda b, n: (rb(b, n), 0)),
        out_shape=jax.ShapeDtypeStruct((batch * seq, M_WIDTH), BF16),
        scratch_shapes=[
            pltpu.VMEM((L + 8, 2 * M_WIDTH), F32),
            pltpu.VMEM((M_HEADS, M_HEAD_DIM, 2 * M_HEAD_DIM), F32),
            pltpu.VMEM((8, 128), F32),
        ],
        compiler_params=pltpu.CompilerParams(
            dimension_semantics=("arbitrary", "arbitrary"),
            vmem_limit_bytes=VMEM_LIMIT),
        name="mlstm",
    )(p, p, p, gates, conv_w, conv_b.reshape(1, -1), gate_bias,
      m_norm_w.reshape(1, -1), m_skip.reshape(1, -1))


def _rope_kernel(q_ref, k_ref, cos_ref, sin_ref, qo_ref, ko_ref, *, q_scale):
    cos = jnp.concatenate([cos_ref[...]] * A_HEADS, axis=1)
    sin = jnp.concatenate([sin_ref[...]] * A_HEADS, axis=1)
    lane = lax.broadcasted_iota(jnp.int32, cos.shape, 1)
    first_half = (lane & (A_QK_DIM - 1)) < (A_QK_DIM // 2)

    def rot(x):
        fwd = pltpu.roll(x, A_QK_DIM // 2, 1)
        bwd = pltpu.roll(x, A_WIDTH - A_QK_DIM // 2, 1)
        return x * cos + jnp.where(first_half, bwd, fwd) * sin

    qo_ref[...] = (rot(q_ref[...].astype(F32)) * q_scale).astype(BF16)
    ko_ref[...] = rot(k_ref[...].astype(F32)).astype(BF16)


def _rope(p, cos_t, sin_t, *, seq, tm=1024):
    rows = p.shape[0]
    nsb = seq // tm
    q_scale = (A_QK_DIM ** -0.5) * LOG2E
    out = jax.ShapeDtypeStruct((rows, A_WIDTH), BF16)
    return pl.pallas_call(
        functools.partial(_rope_kernel, q_scale=q_scale),
        grid=(rows // tm,),
        in_specs=[
            pl.BlockSpec((tm, A_WIDTH), lambda i: (i, 4)),
            pl.BlockSpec((tm, A_WIDTH), lambda i: (i, 5)),
            pl.BlockSpec((tm, 2 * A_QK_DIM), lambda i: (i % nsb, 0)),
            pl.BlockSpec((tm, 2 * A_QK_DIM), lambda i: (i % nsb, 0)),
        ],
        out_specs=[pl.BlockSpec((tm, A_WIDTH), lambda i: (i, 0))] * 2,
        out_shape=[out, out],
        compiler_params=pltpu.CompilerParams(
            dimension_semantics=("arbitrary",), vmem_limit_bytes=VMEM_LIMIT),
        name="rope",
    )(p, p, cos_t, sin_t)


def _attn_kernel(lam_ref, q_ref, k_ref, v_ref, z_ref, nw_ref, o_ref,
                 m_scr, l_scr, acc_scr, *, tq, lambda_init):
    i = pl.program_id(2)
    tk = tq

    q = q_ref[...]
    lane = lax.broadcasted_iota(jnp.int32, q.shape, 1)
    zero = jnp.zeros_like(q)
    q2 = jnp.concatenate([jnp.where(lane < A_QK_DIM, q, zero),
                          jnp.where(lane >= A_QK_DIM, q, zero)], axis=0)

    m_scr[...] = jnp.full(m_scr.shape, NEG_BIG, F32)
    l_scr[...] = jnp.zeros(l_scr.shape, F32)
    acc_scr[...] = jnp.zeros(acc_scr.shape, F32)

    def step(j, masked):
        start = pl.multiple_of(j * tk, tk)
        k = k_ref[pl.ds(start, tk), :]
        v = v_ref[pl.ds(start, tk), :]
        s = lax.dot_general(q2, k, (((1,), (1,)), ((), ())), preferred_element_type=F32)
        if masked:
            r = lax.broadcasted_iota(jnp.int32, s.shape, 0)
            c = lax.broadcasted_iota(jnp.int32, s.shape, 1)
            ok = (c // ATTN_CHUNK) <= ((r % tq) // ATTN_CHUNK)
            s = jnp.where(ok, s, NEG_BIG)
        m_old = m_scr[...]
        m_new = jnp.maximum(m_old, jnp.max(s, axis=1, keepdims=True))
        alpha = jnp.exp2(m_old - m_new)
        p = jnp.exp2(s - m_new)
        l_scr[...] = alpha * l_scr[...] + jnp.sum(p, axis=1, keepdims=True)
        acc_scr[...] = alpha * acc_scr[...] + jnp.dot(p.astype(BF16), v,
                                                      preferred_element_type=F32)
        m_scr[...] = m_new

    def body(j, carry):
        step(j, False)
        return carry

    lax.fori_loop(0, i, body, 0)
    step(i, True)

    lv = lam_ref[...]
    lam = (jnp.exp(jnp.sum(lv[0:1, :] * lv[1:2, :], axis=1, keepdims=True))
           - jnp.exp(jnp.sum(lv[2:3, :] * lv[3:4, :], axis=1, keepdims=True))
           + lambda_init)
    out = acc_scr[0:tq, :] / l_scr[0:tq, :] - lam * (acc_scr[tq:, :] / l_scr[tq:, :])
    ms = jnp.mean(out * out, axis=-1, keepdims=True)
    hn = out * lax.rsqrt(ms + EPS) * nw_ref[...] * (1.0 - lambda_init)
    o_ref[...] = (hn * _silu(z_ref[...].astype(F32))).astype(BF16)


def _attn(lam_vec, qr, kr, p, a_norm_w, *, batch, seq, lambda_init, tq=512):
    nq = seq // tq
    v_col0 = 3072 // A_VDIM
    z_col0 = 3584 // A_VDIM
    return pl.pallas_call(
        functools.partial(_attn_kernel, tq=tq, lambda_init=lambda_init),
        grid=(batch, A_HEADS, nq),
        in_specs=[
            pl.BlockSpec((4, A_QK_DIM), lambda b, h, i: (0, 0)),
            pl.BlockSpec((tq, A_VDIM), lambda b, h, i: (b * nq + i, h)),
            pl.BlockSpec((seq, A_VDIM), lambda b, h, i: (b, h)),
            pl.BlockSpec((seq, A_VDIM), lambda b, h, i: (b, v_col0 + h)),
            pl.BlockSpec((tq, A_VDIM), lambda b, h, i: (b * nq + i, z_col0 + h)),
            pl.BlockSpec((1, A_VDIM), lambda b, h, i: (0, 0)),
        ],
        out_specs=pl.BlockSpec((tq, A_VDIM), lambda b, h, i: (b * nq + i, h)),
        out_shape=jax.ShapeDtypeStruct((batch * seq, A_WIDTH), BF16),
        scratch_shapes=[
            pltpu.VMEM((2 * tq, 1), F32),
            pltpu.VMEM((2 * tq, 1), F32),
            pltpu.VMEM((2 * tq, A_VDIM), F32),
        ],
        compiler_params=pltpu.CompilerParams(
            dimension_semantics=("arbitrary", "arbitrary", "arbitrary"),
            vmem_limit_bytes=VMEM_LIMIT),
        name="diffattn",
    )(lam_vec, qr, kr, p, p, a_norm_w.reshape(1, A_VDIM))


def _outproj_kernel(ym_ref, ya_ref, w_ref, x_ref, fw_ref, o_ref, *, final):
    yc = jnp.concatenate([ym_ref[...], ya_ref[...]], axis=1)
    xn = x_ref[...] + jnp.dot(yc, w_ref[...], preferred_element_type=F32)
    if final:
        ms = jnp.mean(xn * xn, axis=-1, keepdims=True)
        xn = xn * lax.rsqrt(ms + EPS) * fw_ref[...]
    o_ref[...] = xn


def _outproj(ym, ya, w_out, x2, final_w, *, final, tm=1024):
    rows = x2.shape[0]
    return pl.pallas_call(
        functools.partial(_outproj_kernel, final=final),
        grid=(rows // tm,),
        in_specs=[
            pl.BlockSpec((tm, M_WIDTH), lambda i: (i, 0)),
            pl.BlockSpec((tm, A_WIDTH), lambda i: (i, 0)),
            pl.BlockSpec((D_MODEL, D_MODEL), lambda i: (0, 0)),
            pl.BlockSpec((tm, D_MODEL), lambda i: (i, 0)),
            pl.BlockSpec((1, D_MODEL), lambda i: (0, 0)),
        ],
        out_specs=pl.BlockSpec((tm, D_MODEL), lambda i: (i, 0)),
        out_shape=jax.ShapeDtypeStruct((rows, D_MODEL), F32),
        compiler_params=pltpu.CompilerParams(
            dimension_semantics=("arbitrary",), vmem_limit_bytes=VMEM_LIMIT),
        name="outproj",
    )(ym, ya, w_out, x2, final_w.reshape(1, D_MODEL))


def _rope_tables(seq):
    dh = A_QK_DIM
    inv = 1.0 / (ROPE_THETA ** (jnp.arange(0, dh, 2, dtype=F32) / dh))
    ang = jnp.arange(seq, dtype=F32)[:, None] * inv[None, :]
    cos = jnp.concatenate([jnp.cos(ang)] * 4, axis=-1)
    sin = jnp.sin(ang)
    sin = jnp.concatenate([-sin, sin, -sin, sin], axis=-1)
    return cos, sin


def kernel(x, norm_w, w_in, conv_w, conv_b, i_bias, f_bias, m_norm_w, m_skip,
           lam_q1, lam_k1, lam_q2, lam_k2, a_norm_w, w_out, final_norm_w):
    batch, seq, _ = x.shape
    depth = w_in.shape[0]
    x2 = x.reshape(batch * seq, D_MODEL)
    cos_t, sin_t = _rope_tables(seq)
    g0 = 3 * M_WIDTH
    g1 = g0 + 2 * M_HEADS
    for l in range(depth):
        lambda_init = 0.8 - 0.6 * math.exp(-0.3 * l)
        w_main = jnp.concatenate([w_in[l, :, :g0], w_in[l, :, g1:]], axis=1).astype(BF16)
        w_gate = jnp.pad(w_in[l, :, g0:g1], ((0, 0), (0, GATE_LANES - 2 * M_HEADS))).astype(BF16)
        gate_bias = jnp.pad(jnp.concatenate([i_bias[l], f_bias[l]]),
                            (0, GATE_LANES - 2 * M_HEADS)).reshape(1, GATE_LANES)
        lam_vec = jnp.stack([lam_q1[l], lam_k1[l], lam_q2[l], lam_k2[l]]).astype(F32)

        p, gates = _inproj(x2, norm_w[l], w_main, w_gate)
        ym = _mlstm(p, gates, conv_w[l], conv_b[l], gate_bias, m_norm_w[l], m_skip[l],
                    batch=batch, seq=seq)
        qr, kr = _rope(p, cos_t, sin_t, seq=seq)
        ya = _attn(lam_vec, qr, kr, p, a_norm_w[l], batch=batch, seq=seq,
                   lambda_init=lambda_init)
        x2 = _outproj(ym, ya, w_out[l].astype(BF16), x2, final_norm_w,
                      final=(l == depth - 1))
    return x2.reshape(batch, seq, D_MODEL)
```

```python
import functools
import math

import jax
import jax.numpy as jnp
from jax import lax
from jax.experimental import pallas as pl
from jax.experimental.pallas import tpu as pltpu

F32 = jnp.float32
BF16 = jnp.bfloat16

D_MODEL = 1024
M_WIDTH = 512
M_HEADS = 4
M_HEAD_DIM = 128
CONV_WIDTH = 4
A_WIDTH = 512
A_HEADS = 4
A_VDIM = 128
A_QK_DIM = 64
ATTN_CHUNK = 64
ROPE_THETA = 10000.0
EPS = 1e-6
LOG2E = 1.4426950408889634

P_WIDTH = 4096
GATE_LANES = 128

VMEM_LIMIT = 56 * 1024 * 1024
NEG_BIG = -1e30


def _silu(y):
    return y * (1.0 / (1.0 + jnp.exp(-y)))


def _inproj_kernel(x_ref, nw_ref, w_ref, wg_ref, p_ref, g_ref, h_scr):
    @pl.when(pl.program_id(1) == 0)
    def _():
        x = x_ref[...]
        ms = jnp.mean(x * x, axis=-1, keepdims=True)
        h = (x * lax.rsqrt(ms + EPS) * nw_ref[...]).astype(BF16)
        h_scr[...] = h
        g_ref[...] = jnp.dot(h, wg_ref[...], preferred_element_type=F32)

    p_ref[...] = jnp.dot(h_scr[...], w_ref[...], preferred_element_type=F32).astype(BF16)


def _inproj(x2, norm_w, w_main, w_gate, *, tm=1024, tn=1024):
    rows = x2.shape[0]
    return pl.pallas_call(
        _inproj_kernel,
        grid=(rows // tm, P_WIDTH // tn),
        in_specs=[
            pl.BlockSpec((tm, D_MODEL), lambda i, j: (i, 0)),
            pl.BlockSpec((1, D_MODEL), lambda i, j: (0, 0)),
            pl.BlockSpec((D_MODEL, tn), lambda i, j: (0, j)),
            pl.BlockSpec((D_MODEL, GATE_LANES), lambda i, j: (0, 0)),
        ],
        out_specs=[
            pl.BlockSpec((tm, tn), lambda i, j: (i, j)),
            pl.BlockSpec((tm, GATE_LANES), lambda i, j: (i, 0)),
        ],
        out_shape=[
            jax.ShapeDtypeStruct((rows, P_WIDTH), BF16),
            jax.ShapeDtypeStruct((rows, GATE_LANES), F32),
        ],
        scratch_shapes=[pltpu.VMEM((tm, D_MODEL), BF16)],
        compiler_params=pltpu.CompilerParams(
            dimension_semantics=("arbitrary", "arbitrary"),
            vmem_limit_bytes=VMEM_LIMIT),
        name="inproj",
    )(x2, norm_w.reshape(1, D_MODEL), w_main, w_gate)


def _dot_f32_exact_lhs01(tri, x):
    hi = x.astype(BF16)
    r1 = x - hi.astype(F32)
    mid = r1.astype(BF16)
    lo = (r1 - mid.astype(F32)).astype(BF16)
    d = lambda t: jnp.dot(tri, t, preferred_element_type=F32)
    return d(hi) + d(mid) + d(lo)


def _mlstm_kernel(qk_ref, v_ref, z_ref, g_ref, cw_ref, cb_ref, gb_ref, nw_ref, sk_ref,
                  o_ref, xbuf, c_scr, m_scr, *, L):
    n = pl.program_id(1)
    D = M_HEAD_DIM

    @pl.when(n == 0)
    def _():
        xbuf[0:8, :] = jnp.zeros((8, 2 * M_WIDTH), F32)
        c_scr[...] = jnp.zeros(c_scr.shape, F32)
        m_scr[...] = jnp.zeros(m_scr.shape, F32)

    xbuf[8:8 + L, :] = qk_ref[...].astype(F32)
    y = cb_ref[...] + cw_ref[3:4, :] * xbuf[8:8 + L, :]
    y = y + cw_ref[2:3, :] * xbuf[7:7 + L, :]
    y = y + cw_ref[1:2, :] * xbuf[6:6 + L, :]
    y = y + cw_ref[0:1, :] * xbuf[5:5 + L, :]
    xbuf[0:8, :] = xbuf[L:L + 8, :]
    qkc = _silu(y)

    g = g_ref[...] + gb_ref[...]
    lf = jnp.minimum(g, 0.0) - jnp.log1p(jnp.exp(-jnp.abs(g)))
    row = lax.broadcasted_iota(jnp.int32, (L, L), 0)
    col = lax.broadcasted_iota(jnp.int32, (L, L), 1)
    causal = col <= row
    tri = jnp.where(causal, 1.0, 0.0).astype(BF16)
    bcum = _dot_f32_exact_lhs01(tri, lf)
    lane = lax.broadcasted_iota(jnp.int32, (L, GATE_LANES), 1)
    a_col = jnp.where(lane < M_HEADS, g, bcum)
    a_row = a_col.T

    ones = jnp.ones((L, D), BF16)
    for h in range(M_HEADS):
        sl = slice(h * D, (h + 1) * D)
        b_c = a_col[:, M_HEADS + h:M_HEADS + h + 1]
        li_c = a_col[:, h:h + 1]
        b_r = a_row[M_HEADS + h:M_HEADS + h + 1, :]
        li_r = a_row[h:h + 1, :]
        m_prev = m_scr[h:h + 1, 0:1]

        dm = jnp.where(causal, b_c - b_r + li_r, NEG_BIG)
        inter = b_c + m_prev
        m_i = jnp.maximum(inter, jnp.max(dm, axis=1, keepdims=True))
        w_intra = jnp.exp(dm - m_i)
        w_inter = jnp.exp(inter - m_i)

        qh = qkc[:, sl]
        kh = qkc[:, M_WIDTH + h * D:M_WIDTH + (h + 1) * D] * (D ** -0.5)
        qb = qh.astype(BF16)
        v_ext = jnp.concatenate([v_ref[:, sl], ones], axis=1)
        s = lax.dot_general(qb, kh.astype(BF16), (((1,), (1,)), ((), ())),
                            preferred_element_type=F32) * w_intra
        c_old = c_scr[h]
        q_c = jnp.dot(qb, c_old.astype(BF16), preferred_element_type=F32)
        s_v = jnp.dot(s.astype(BF16), v_ext, preferred_element_type=F32)
        num = w_inter * q_c[:, :D] + s_v[:, :D]
        den = w_inter * q_c[:, D:D + 1] + s_v[:, D:D + 1]
        hm = num / jnp.maximum(jnp.abs(den), jnp.exp(-m_i))

        b_last = b_c[L - 1:L, :]
        gg = b_last - b_c + li_c
        m_new = jnp.maximum(b_last + m_prev, jnp.max(gg, axis=0, keepdims=True))
        decay = jnp.exp(b_last + m_prev - m_new)
        wk = jnp.exp(gg - m_new)
        kw_t = (kh * wk).T.astype(BF16)
        c_scr[h] = decay * c_old + jnp.dot(kw_t, v_ext, preferred_element_type=F32)
        m_scr[h:h + 1, :] = jnp.broadcast_to(m_new, (1, m_scr.shape[1]))

        ms = jnp.mean(hm * hm, axis=-1, keepdims=True)
        hn = hm * lax.rsqrt(ms + EPS) * nw_ref[:, sl]
        ym = (hn + sk_ref[:, sl] * qh) * _silu(z_ref[:, sl].astype(F32))
        o_ref[:, sl] = ym.astype(BF16)


def _mlstm(p, gates, conv_w, conv_b, gate_bias, m_norm_w, m_skip, *, batch, seq, L=256):
    nchunk = seq // L
    rb = lambda b, n: b * nchunk + n
    return pl.pallas_call(
        functools.partial(_mlstm_kernel, L=L),
        grid=(batch, nchunk),
        in_specs=[
            pl.BlockSpec((L, 2 * M_WIDTH), lambda b, n: (rb(b, n), 0)),
            pl.BlockSpec((L, M_WIDTH), lambda b, n: (rb(b, n), 2)),
            pl.BlockSpec((L, M_WIDTH), lambda b, n: (rb(b, n), 3)),
            pl.BlockSpec((L, GATE_LANES), lambda b, n: (rb(b, n), 0)),
            pl.BlockSpec((CONV_WIDTH, 2 * M_WIDTH), lambda b, n: (0, 0)),
            pl.BlockSpec((1, 2 * M_WIDTH), lambda b, n: (0, 0)),
            pl.BlockSpec((1, GATE_LANES), lambda b, n: (0, 0)),
            pl.BlockSpec((1, M_WIDTH), lambda b, n: (0, 0)),
            pl.BlockSpec((1, M_WIDTH), lambda b, n: (0, 0)),
        ],
        out_specs=pl.BlockSpec((L, M_WIDTH), lambda b, n: (rb(b, n), 0)),
        out_shape=jax.ShapeDtypeStruct((batch * seq, M_WIDTH), BF16),
        scratch_shapes=[
            pltpu.VMEM((L + 8, 2 * M_WIDTH), F32),
            pltpu.VMEM((M_HEADS, M_HEAD_DIM, 2 * M_HEAD_DIM), F32),
            pltpu.VMEM((8, 128), F32),
        ],
        compiler_params=pltpu.CompilerParams(
            dimension_semantics=("arbitrary", "arbitrary"),
            vmem_limit_bytes=VMEM_LIMIT),
        name="mlstm",
    )(p, p, p, gates, conv_w, conv_b.reshape(1, -1), gate_bias,
      m_norm_w.reshape(1, -1), m_skip.reshape(1, -1))


def _rope_kernel(q_ref, k_ref, v_ref, cos_ref, sin_ref, qt_ref, ko_ref, vt_ref, *, q_scale):
    cos = jnp.concatenate([cos_ref[...]] * A_HEADS, axis=1)
    sin = jnp.concatenate([sin_ref[...]] * A_HEADS, axis=1)
    lane = lax.broadcasted_iota(jnp.int32, cos.shape, 1)
    first_half = (lane & (A_QK_DIM - 1)) < (A_QK_DIM // 2)

    def rot(x):
        fwd = pltpu.roll(x, A_QK_DIM // 2, 1)
        bwd = pltpu.roll(x, A_WIDTH - A_QK_DIM // 2, 1)
        return x * cos + jnp.where(first_half, bwd, fwd) * sin

    qt_ref[...] = (rot(q_ref[...].astype(F32)) * q_scale).T.astype(BF16)
    ko_ref[...] = rot(k_ref[...].astype(F32)).astype(BF16)
    vt_ref[...] = v_ref[...].astype(F32).T.astype(BF16)


def _rope(p, cos_t, sin_t, *, batch, seq, tm=1024):
    rows = p.shape[0]
    nsb = seq // tm
    q_scale = (A_QK_DIM ** -0.5) * LOG2E
    nat = jax.ShapeDtypeStruct((rows, A_WIDTH), BF16)
    tr = jax.ShapeDtypeStruct((batch * A_WIDTH, seq), BF16)
    tr_spec = pl.BlockSpec((A_WIDTH, tm), lambda i: (i // nsb, i % nsb))
    return pl.pallas_call(
        functools.partial(_rope_kernel, q_scale=q_scale),
        grid=(rows // tm,),
        in_specs=[
            pl.BlockSpec((tm, A_WIDTH), lambda i: (i, 4)),
            pl.BlockSpec((tm, A_WIDTH), lambda i: (i, 5)),
            pl.BlockSpec((tm, A_WIDTH), lambda i: (i, 6)),
            pl.BlockSpec((tm, 2 * A_QK_DIM), lambda i: (i % nsb, 0)),
            pl.BlockSpec((tm, 2 * A_QK_DIM), lambda i: (i % nsb, 0)),
        ],
        out_specs=[tr_spec, pl.BlockSpec((tm, A_WIDTH), lambda i: (i, 0)), tr_spec],
        out_shape=[tr, nat, tr],
        compiler_params=pltpu.CompilerParams(
            dimension_semantics=("arbitrary",), vmem_limit_bytes=VMEM_LIMIT),
        name="rope",
    )(p, p, p, cos_t, sin_t)


ONES_ROWS = 16
QGROUP = 256


def _attn_kernel(lam_ref, qt_ref, k_ref, vt_ref, z_ref, nw_ref, o_ref,
                 m_scr, acc_scr, sa, sb, cma, cmb, *, tq, tk, lambda_init):
    i = pl.program_id(2)
    dv = A_VDIM
    ng = 2 * tq // QGROUP

    qt = qt_ref[...]
    rowi = lax.broadcasted_iota(jnp.int32, qt.shape, 0)
    zero = jnp.zeros_like(qt)
    q2t = jnp.concatenate([jnp.where(rowi < A_QK_DIM, qt, zero),
                           jnp.where(rowi >= A_QK_DIM, qt, zero)], axis=1)

    m_scr[...] = jnp.full(m_scr.shape, NEG_BIG, F32)
    acc_scr[...] = jnp.zeros(acc_scr.shape, F32)
    ones = jnp.ones((ONES_ROWS, tk), BF16)

    def scores_group(start, s_ref, cm_ref, diag_off, g):
        gs = slice(g * QGROUP, (g + 1) * QGROUP)
        qlo = (g * QGROUP) % tq
        if diag_off is not None and diag_off >= qlo + QGROUP:
            s_ref[:, gs] = jnp.full((tk, QGROUP), NEG_BIG, F32)
            cm_ref[:, gs] = jnp.full((1, QGROUP), NEG_BIG, F32)
            return
        k = k_ref[pl.ds(start, tk), :]
        st = jnp.dot(k, q2t[:, gs], preferred_element_type=F32)
        if diag_off is not None and diag_off + tk > qlo + ATTN_CHUNK:
            kk = lax.broadcasted_iota(jnp.int32, st.shape, 0) + diag_off
            qq = lax.broadcasted_iota(jnp.int32, st.shape, 1) + qlo
            st = jnp.where((kk // ATTN_CHUNK) <= (qq // ATTN_CHUNK), st, NEG_BIG)
        s_ref[:, gs] = st
        cm_ref[:, gs] = jnp.max(st, axis=0, keepdims=True)

    def consume_group(start, s_ref, cm_ref, g):
        gs = slice(g * QGROUP, (g + 1) * QGROUP)
        vt_ext = jnp.concatenate([vt_ref[:, pl.ds(start, tk)], ones], axis=0)
        m_old = m_scr[:, gs]
        m_new = jnp.maximum(m_old, cm_ref[:, gs])
        alpha = jnp.exp2(m_old - m_new)
        pt = jnp.exp2(s_ref[:, gs] - m_new).astype(BF16)
        acc_scr[:, gs] = alpha * acc_scr[:, gs] + jnp.dot(vt_ext, pt,
                                                          preferred_element_type=F32)
        m_scr[:, gs] = m_new

    def stage(score_args, consume_args):
        for g in range(ng):
            if score_args is not None:
                scores_group(*score_args, g)
            if consume_args is not None:
                consume_group(*consume_args, g)

    d0 = pl.multiple_of(i * tq, tq)
    d1 = pl.multiple_of(i * tq + tk, tk)
    stage((d0, sa, cma, 0), None)
    stage((d1, sb, cmb, tk), (d0, sa, cma))

    def body(t, pending_b):
        u0 = pl.multiple_of(2 * t * tk, tk)
        u1 = pl.multiple_of(u0 + tk, tk)
        stage((u0, sa, cma, None), (pl.multiple_of(pending_b, tk), sb, cmb))
        stage((u1, sb, cmb, None), (u0, sa, cma))
        return u1

    pending_b = lax.fori_loop(0, i, body, d1)
    stage(None, (pl.multiple_of(pending_b, tk), sb, cmb))

    lv = lam_ref[...]
    lam = (jnp.exp(jnp.sum(lv[0:1, :] * lv[1:2, :], axis=1, keepdims=True))
           - jnp.exp(jnp.sum(lv[2:3, :] * lv[3:4, :], axis=1, keepdims=True))
           + lambda_init)
    o0 = acc_scr[0:dv, 0:tq] / acc_scr[dv:dv + 1, 0:tq]
    o1 = acc_scr[0:dv, tq:] / acc_scr[dv:dv + 1, tq:]
    out_t = o0 - lam * o1
    ms = jnp.mean(out_t * out_t, axis=0, keepdims=True)
    hn = (out_t * lax.rsqrt(ms + EPS)).T
    hn = hn * nw_ref[...] * (1.0 - lambda_init)
    o_ref[...] = (hn * _silu(z_ref[...].astype(F32))).astype(BF16)


def _attn(lam_vec, qt, kr, vt, p, a_norm_w, *, batch, seq, lambda_init, tq=512):
    nq = seq // tq
    tk = tq // 2
    z_col0 = 3584 // A_VDIM
    return pl.pallas_call(
        functools.partial(_attn_kernel, tq=tq, tk=tk, lambda_init=lambda_init),
        grid=(batch, A_HEADS, nq),
        in_specs=[
            pl.BlockSpec((4, A_QK_DIM), lambda b, h, i: (0, 0)),
            pl.BlockSpec((A_VDIM, tq), lambda b, h, i: (b * A_HEADS + h, i)),
            pl.BlockSpec((seq, A_VDIM), lambda b, h, i: (b, h)),
            pl.BlockSpec((A_VDIM, seq), lambda b, h, i: (b * A_HEADS + h, 0)),
            pl.BlockSpec((tq, A_VDIM), lambda b, h, i: (b * nq + i, z_col0 + h)),
            pl.BlockSpec((1, A_VDIM), lambda b, h, i: (0, 0)),
        ],
        out_specs=pl.BlockSpec((tq, A_VDIM), lambda b, h, i: (b * nq + i, h)),
        out_shape=jax.ShapeDtypeStruct((batch * seq, A_WIDTH), BF16),
        scratch_shapes=[
            pltpu.VMEM((1, 2 * tq), F32),
            pltpu.VMEM((A_VDIM + ONES_ROWS, 2 * tq), F32),
            pltpu.VMEM((tk, 2 * tq), F32),
            pltpu.VMEM((tk, 2 * tq), F32),
            pltpu.VMEM((1, 2 * tq), F32),
            pltpu.VMEM((1, 2 * tq), F32),
        ],
        compiler_params=pltpu.CompilerParams(
            dimension_semantics=("arbitrary", "arbitrary", "arbitrary"),
            vmem_limit_bytes=VMEM_LIMIT),
        name="diffattn",
    )(lam_vec, qt, kr, vt, p, a_norm_w.reshape(1, A_VDIM))


def _outproj_kernel(ym_ref, ya_ref, w_ref, x_ref, fw_ref, o_ref, *, final):
    yc = jnp.concatenate([ym_ref[...], ya_ref[...]], axis=1)
    xn = x_ref[...] + jnp.dot(yc, w_ref[...], preferred_element_type=F32)
    if final:
        ms = jnp.mean(xn * xn, axis=-1, keepdims=True)
        xn = xn * lax.rsqrt(ms + EPS) * fw_ref[...]
    o_ref[...] = xn


def _outproj(ym, ya, w_out, x2, final_w, *, final, tm=1024):
    rows = x2.shape[0]
    return pl.pallas_call(
        functools.partial(_outproj_kernel, final=final),
        grid=(rows // tm,),
        in_specs=[
            pl.BlockSpec((tm, M_WIDTH), lambda i: (i, 0)),
            pl.BlockSpec((tm, A_WIDTH), lambda i: (i, 0)),
            pl.BlockSpec((D_MODEL, D_MODEL), lambda i: (0, 0)),
            pl.BlockSpec((tm, D_MODEL), lambda i: (i, 0)),
            pl.BlockSpec((1, D_MODEL), lambda i: (0, 0)),
        ],
        out_specs=pl.BlockSpec((tm, D_MODEL), lambda i: (i, 0)),
        out_shape=jax.ShapeDtypeStruct((rows, D_MODEL), F32),
        compiler_params=pltpu.CompilerParams(
            dimension_semantics=("arbitrary",), vmem_limit_bytes=VMEM_LIMIT),
        name="outproj",
    )(ym, ya, w_out, x2, final_w.reshape(1, D_MODEL))


def _rope_tables(seq):
    dh = A_QK_DIM
    inv = 1.0 / (ROPE_THETA ** (jnp.arange(0, dh, 2, dtype=F32) / dh))
    ang = jnp.arange(seq, dtype=F32)[:, None] * inv[None, :]
    cos = jnp.concatenate([jnp.cos(ang)] * 4, axis=-1)
    sin = jnp.sin(ang)
    sin = jnp.concatenate([-sin, sin, -sin, sin], axis=-1)
    return cos, sin


def kernel(x, norm_w, w_in, conv_w, conv_b, i_bias, f_bias, m_norm_w, m_skip,
           lam_q1, lam_k1, lam_q2, lam_k2, a_norm_w, w_out, final_norm_w):
    batch, seq, _ = x.shape
    depth = w_in.shape[0]
    x2 = x.reshape(batch * seq, D_MODEL)
    cos_t, sin_t = _rope_tables(seq)
    g0 = 3 * M_WIDTH
    g1 = g0 + 2 * M_HEADS
    for l in range(depth):
        lambda_init = 0.8 - 0.6 * math.exp(-0.3 * l)
        w_main = jnp.concatenate([w_in[l, :, :g0], w_in[l, :, g1:]], axis=1).astype(BF16)
        w_gate = jnp.pad(w_in[l, :, g0:g1], ((0, 0), (0, GATE_LANES - 2 * M_HEADS))).astype(BF16)
        gate_bias = jnp.pad(jnp.concatenate([i_bias[l], f_bias[l]]),
                            (0, GATE_LANES - 2 * M_HEADS)).reshape(1, GATE_LANES)
        lam_vec = jnp.stack([lam_q1[l], lam_k1[l], lam_q2[l], lam_k2[l]]).astype(F32)

        p, gates = _inproj(x2, norm_w[l], w_main, w_gate)
        ym = _mlstm(p, gates, conv_w[l], conv_b[l], gate_bias, m_norm_w[l], m_skip[l],
                    batch=batch, seq=seq)
        qt, kr, vt = _rope(p, cos_t, sin_t, batch=batch, seq=seq)
        ya = _attn(lam_vec, qt, kr, vt, p, a_norm_w[l], batch=batch, seq=seq,
                   lambda_init=lambda_init)
        x2 = _outproj(ym, ya, w_out[l].astype(BF16), x2, final_norm_w,
                      final=(l == depth - 1))
    return x2.reshape(batch, seq, D_MODEL)
```

```python
import functools
import math

import jax
import jax.numpy as jnp
from jax import lax
from jax.experimental import pallas as pl
from jax.experimental.pallas import tpu as pltpu

F32 = jnp.float32
BF16 = jnp.bfloat16

D_MODEL = 1024
M_WIDTH = 512
M_HEADS = 4
M_HEAD_DIM = 128
CONV_WIDTH = 4
A_WIDTH = 512
A_HEADS = 4
A_VDIM = 128
A_QK_DIM = 64
ATTN_CHUNK = 64
ROPE_THETA = 10000.0
EPS = 1e-6
LOG2E = 1.4426950408889634

P_WIDTH = 4096
GATE_LANES = 128

VMEM_LIMIT = 56 * 1024 * 1024
NEG_BIG = -1e30


def _silu(y):
    hy = 0.5 * y
    return hy + hy * jnp.tanh(hy)


def _inproj_kernel(x_ref, nw_ref, w_ref, wg_ref, p_ref, g_ref, h_scr):
    @pl.when(pl.program_id(1) == 0)
    def _():
        x = x_ref[...]
        ms = jnp.mean(x * x, axis=-1, keepdims=True)
        h = (x * lax.rsqrt(ms + EPS) * nw_ref[...]).astype(BF16)
        h_scr[...] = h
        g_ref[...] = jnp.dot(h, wg_ref[...], preferred_element_type=F32)

    p_ref[...] = jnp.dot(h_scr[...], w_ref[...], preferred_element_type=F32).astype(BF16)


def _inproj(x2, norm_w, w_main, w_gate, *, tm=1024, tn=1024):
    rows = x2.shape[0]
    return pl.pallas_call(
        _inproj_kernel,
        grid=(rows // tm, P_WIDTH // tn),
        in_specs=[
            pl.BlockSpec((tm, D_MODEL), lambda i, j: (i, 0)),
            pl.BlockSpec((1, D_MODEL), lambda i, j: (0, 0)),
            pl.BlockSpec((D_MODEL, tn), lambda i, j: (0, j)),
            pl.BlockSpec((D_MODEL, GATE_LANES), lambda i, j: (0, 0)),
        ],
        out_specs=[
            pl.BlockSpec((tm, tn), lambda i, j: (i, j)),
            pl.BlockSpec((tm, GATE_LANES), lambda i, j: (i, 0)),
        ],
        out_shape=[
            jax.ShapeDtypeStruct((rows, P_WIDTH), BF16),
            jax.ShapeDtypeStruct((rows, GATE_LANES), F32),
        ],
        scratch_shapes=[pltpu.VMEM((tm, D_MODEL), BF16)],
        compiler_params=pltpu.CompilerParams(
            dimension_semantics=("arbitrary", "arbitrary"),
            vmem_limit_bytes=VMEM_LIMIT),
        name="inproj",
    )(x2, norm_w.reshape(1, D_MODEL), w_main, w_gate)


def _dot_f32_exact_lhs01(tri, x):
    hi = x.astype(BF16)
    r1 = x - hi.astype(F32)
    mid = r1.astype(BF16)
    lo = (r1 - mid.astype(F32)).astype(BF16)
    d = lambda t: jnp.dot(tri, t, preferred_element_type=F32)
    return d(hi) + d(mid) + d(lo)


def _mlstm_kernel(qk_ref, v_ref, z_ref, g_ref, cw_ref, cb_ref, gb_ref, nw_ref, sk_ref,
                  o_ref, xbuf, c_scr, m_scr, *, L):
    n = pl.program_id(1)
    D = M_HEAD_DIM

    @pl.when(n == 0)
    def _():
        xbuf[0:8, :] = jnp.zeros((8, 2 * M_WIDTH), F32)
        c_scr[...] = jnp.zeros(c_scr.shape, F32)
        m_scr[...] = jnp.zeros(m_scr.shape, F32)

    xbuf[8:8 + L, :] = qk_ref[...].astype(F32)
    y = cb_ref[...] + cw_ref[3:4, :] * xbuf[8:8 + L, :]
    y = y + cw_ref[2:3, :] * xbuf[7:7 + L, :]
    y = y + cw_ref[1:2, :] * xbuf[6:6 + L, :]
    y = y + cw_ref[0:1, :] * xbuf[5:5 + L, :]
    xbuf[0:8, :] = xbuf[L:L + 8, :]
    qkc = _silu(y)

    g = g_ref[...] + gb_ref[...]
    lf = jnp.minimum(g, 0.0) - jnp.log1p(jnp.exp(-jnp.abs(g)))
    row = lax.broadcasted_iota(jnp.int32, (L, L), 0)
    col = lax.broadcasted_iota(jnp.int32, (L, L), 1)
    causal = col <= row
    tri = jnp.where(causal, 1.0, 0.0).astype(BF16)
    bcum = _dot_f32_exact_lhs01(tri, lf)
    lane = lax.broadcasted_iota(jnp.int32, (L, GATE_LANES), 1)
    a_col = jnp.where(lane < M_HEADS, g, bcum)
    a_row = a_col.T

    ones = jnp.ones((L, D), BF16)
    for h in range(M_HEADS):
        sl = slice(h * D, (h + 1) * D)
        b_c = a_col[:, M_HEADS + h:M_HEADS + h + 1]
        li_c = a_col[:, h:h + 1]
        b_r = a_row[M_HEADS + h:M_HEADS + h + 1, :]
        li_r = a_row[h:h + 1, :]
        m_prev = m_scr[h:h + 1, 0:1]

        dm = jnp.where(causal, b_c - b_r + li_r, NEG_BIG)
        inter = b_c + m_prev
        m_i = jnp.maximum(inter, jnp.max(dm, axis=1, keepdims=True))
        w_intra = jnp.exp(dm - m_i)
        w_inter = jnp.exp(inter - m_i)

        qh = qkc[:, sl]
        kh = qkc[:, M_WIDTH + h * D:M_WIDTH + (h + 1) * D] * (D ** -0.5)
        qb = qh.astype(BF16)
        v_ext = jnp.concatenate([v_ref[:, sl], ones], axis=1)
        s = lax.dot_general(qb, kh.astype(BF16), (((1,), (1,)), ((), ())),
                            preferred_element_type=F32) * w_intra
        c_old = c_scr[h]
        q_c = jnp.dot(qb, c_old.astype(BF16), preferred_element_type=F32)
        s_v = jnp.dot(s.astype(BF16), v_ext, preferred_element_type=F32)
        num = w_inter * q_c[:, :D] + s_v[:, :D]
        den = w_inter * q_c[:, D:D + 1] + s_v[:, D:D + 1]
        hm = num / jnp.maximum(jnp.abs(den), jnp.exp(-m_i))

        b_last = b_c[L - 1:L, :]
        gg = b_last - b_c + li_c
        m_new = jnp.maximum(b_last + m_prev, jnp.max(gg, axis=0, keepdims=True))
        decay = jnp.exp(b_last + m_prev - m_new)
        wk = jnp.exp(gg - m_new)
        kw_t = (kh * wk).T.astype(BF16)
        c_scr[h] = decay * c_old + jnp.dot(kw_t, v_ext, preferred_element_type=F32)
        m_scr[h:h + 1, :] = jnp.broadcast_to(m_new, (1, m_scr.shape[1]))

        ms = jnp.mean(hm * hm, axis=-1, keepdims=True)
        hn = hm * lax.rsqrt(ms + EPS) * nw_ref[:, sl]
        ym = (hn + sk_ref[:, sl] * qh) * _silu(z_ref[:, sl].astype(F32))
        o_ref[:, sl] = ym.astype(BF16)


def _mlstm(p, gates, conv_w, conv_b, gate_bias, m_norm_w, m_skip, *, batch, seq, L=256):
    nchunk = seq // L
    rb = lambda b, n: b * nchunk + n
    return pl.pallas_call(
        functools.partial(_mlstm_kernel, L=L),
        grid=(batch, nchunk),
        in_specs=[
            pl.BlockSpec((L, 2 * M_WIDTH), lambda b, n: (rb(b, n), 0)),
            pl.BlockSpec((L, M_WIDTH), lambda b, n: (rb(b, n), 2)),
            pl.BlockSpec((L, M_WIDTH), lambda b, n: (rb(b, n), 3)),
            pl.BlockSpec((L, GATE_LANES), lambda b, n: (rb(b, n), 0)),
            pl.BlockSpec((CONV_WIDTH, 2 * M_WIDTH), lambda b, n: (0, 0)),
            pl.BlockSpec((1, 2 * M_WIDTH), lambda b, n: (0, 0)),
            pl.BlockSpec((1, GATE_LANES), lambda b, n: (0, 0)),
            pl.BlockSpec((1, M_WIDTH), lambda b, n: (0, 0)),
            pl.BlockSpec((1, M_WIDTH), lambda b, n: (0, 0)),
        ],
        out_specs=pl.BlockSpec((L, M_WIDTH), lambda b, n: (rb(b, n), 0)),
        out_shape=jax.ShapeDtypeStruct((batch * seq, M_WIDTH), BF16),
        scratch_shapes=[
            pltpu.VMEM((L + 8, 2 * M_WIDTH), F32),
            pltpu.VMEM((M_HEADS, M_HEAD_DIM, 2 * M_HEAD_DIM), F32),
            pltpu.VMEM((8, 128), F32),
        ],
        compiler_params=pltpu.CompilerParams(
            dimension_semantics=("arbitrary", "arbitrary"),
            vmem_limit_bytes=VMEM_LIMIT),
        name="mlstm",
    )(p, p, p, gates, conv_w, conv_b.reshape(1, -1), gate_bias,
      m_norm_w.reshape(1, -1), m_skip.reshape(1, -1))


def _rope_kernel(q_ref, k_ref, v_ref, cos_ref, sin_ref, qt_ref, ko_ref, vt_ref, *, q_scale):
    cos = jnp.concatenate([cos_ref[...]] * A_HEADS, axis=1)
    sin = jnp.concatenate([sin_ref[...]] * A_HEADS, axis=1)
    lane = lax.broadcasted_iota(jnp.int32, cos.shape, 1)
    first_half = (lane & (A_QK_DIM - 1)) < (A_QK_DIM // 2)

    def rot(x):
        fwd = pltpu.roll(x, A_QK_DIM // 2, 1)
        bwd = pltpu.roll(x, A_WIDTH - A_QK_DIM // 2, 1)
        return x * cos + jnp.where(first_half, bwd, fwd) * sin

    qt_ref[...] = (rot(q_ref[...].astype(F32)) * q_scale).T.astype(BF16)
    ko_ref[...] = rot(k_ref[...].astype(F32)).astype(BF16)
    vt_ref[...] = v_ref[...].astype(F32).T.astype(BF16)


def _rope(p, cos_t, sin_t, *, batch, seq, tm=1024):
    rows = p.shape[0]
    nsb = seq // tm
    q_scale = (A_QK_DIM ** -0.5) * LOG2E
    nat = jax.ShapeDtypeStruct((rows, A_WIDTH), BF16)
    tr = jax.ShapeDtypeStruct((batch * A_WIDTH, seq), BF16)
    tr_spec = pl.BlockSpec((A_WIDTH, tm), lambda i: (i // nsb, i % nsb))
    return pl.pallas_call(
        functools.partial(_rope_kernel, q_scale=q_scale),
        grid=(rows // tm,),
        in_specs=[
            pl.BlockSpec((tm, A_WIDTH), lambda i: (i, 4)),
            pl.BlockSpec((tm, A_WIDTH), lambda i: (i, 5)),
            pl.BlockSpec((tm, A_WIDTH), lambda i: (i, 6)),
            pl.BlockSpec((tm, 2 * A_QK_DIM), lambda i: (i % nsb, 0)),
            pl.BlockSpec((tm, 2 * A_QK_DIM), lambda i: (i % nsb, 0)),
        ],
        out_specs=[tr_spec, pl.BlockSpec((tm, A_WIDTH), lambda i: (i, 0)), tr_spec],
        out_shape=[tr, nat, tr],
        compiler_params=pltpu.CompilerParams(
            dimension_semantics=("arbitrary",), vmem_limit_bytes=VMEM_LIMIT),
        name="rope",
    )(p, p, p, cos_t, sin_t)


ONES_ROWS = 16
QGROUP = 256


def _attn_kernel(lam_ref, qt_ref, k_ref, vt_ref, z_ref, nw_ref, o_ref,
                 m_scr, acc_scr, sa, sb, cma, cmb, *, tq, tk, lambda_init):
    i = pl.program_id(2)
    dv = A_VDIM
    ng = 2 * tq // QGROUP

    qt = qt_ref[...]
    rowi = lax.broadcasted_iota(jnp.int32, qt.shape, 0)
    zero = jnp.zeros_like(qt)
    q2t = jnp.concatenate([jnp.where(rowi < A_QK_DIM, qt, zero),
                           jnp.where(rowi >= A_QK_DIM, qt, zero)], axis=1)

    m_scr[...] = jnp.full(m_scr.shape, NEG_BIG, F32)
    acc_scr[...] = jnp.zeros(acc_scr.shape, F32)
    ones = jnp.ones((ONES_ROWS, tk), BF16)

    def scores_group(start, s_ref, cm_ref, diag_off, g):
        gs = slice(g * QGROUP, (g + 1) * QGROUP)
        qlo = (g * QGROUP) % tq
        if diag_off is not None and diag_off >= qlo + QGROUP:
            s_ref[:, gs] = jnp.full((tk, QGROUP), NEG_BIG, F32)
            cm_ref[:, gs] = jnp.full((1, QGROUP), NEG_BIG, F32)
            return
        k = k_ref[pl.ds(start, tk), :]
        st = jnp.dot(k, q2t[:, gs], preferred_element_type=F32)
        if diag_off is not None and diag_off + tk > qlo + ATTN_CHUNK:
            kk = lax.broadcasted_iota(jnp.int32, st.shape, 0) + diag_off
            qq = lax.broadcasted_iota(jnp.int32, st.shape, 1) + qlo
            st = jnp.where((kk // ATTN_CHUNK) <= (qq // ATTN_CHUNK), st, NEG_BIG)
        s_ref[:, gs] = st
        cm_ref[:, gs] = jnp.max(st, axis=0, keepdims=True)

    def consume_group(start, s_ref, cm_ref, g):
        gs = slice(g * QGROUP, (g + 1) * QGROUP)
        vt_ext = jnp.concatenate([vt_ref[:, pl.ds(start, tk)], ones], axis=0)
        m_old = m_scr[:, gs]
        m_new = jnp.maximum(m_old, cm_ref[:, gs])
        alpha = jnp.exp2(m_old - m_new)
        pt = jnp.exp2((s_ref[:, gs] - m_new).astype(BF16))
        acc_scr[:, gs] = alpha * acc_scr[:, gs] + jnp.dot(vt_ext, pt,
                                                          preferred_element_type=F32)
        m_scr[:, gs] = m_new

    def stage(score_args, consume_args):
        for g in range(ng):
            if score_args is not None:
                scores_group(*score_args, g)
            if consume_args is not None:
                consume_group(*consume_args, g)

    d0 = pl.multiple_of(i * tq, tq)
    d1 = pl.multiple_of(i * tq + tk, tk)
    stage((d0, sa, cma, 0), None)
    stage((d1, sb, cmb, tk), (d0, sa, cma))

    def body(t, pending_b):
        u0 = pl.multiple_of(2 * t * tk, tk)
        u1 = pl.multiple_of(u0 + tk, tk)
        stage((u0, sa, cma, None), (pl.multiple_of(pending_b, tk), sb, cmb))
        stage((u1, sb, cmb, None), (u0, sa, cma))
        return u1

    pending_b = lax.fori_loop(0, i, body, d1)
    stage(None, (pl.multiple_of(pending_b, tk), sb, cmb))

    lv = lam_ref[...]
    lam = (jnp.exp(jnp.sum(lv[0:1, :] * lv[1:2, :], axis=1, keepdims=True))
           - jnp.exp(jnp.sum(lv[2:3, :] * lv[3:4, :], axis=1, keepdims=True))
           + lambda_init)
    o0 = acc_scr[0:dv, 0:tq] / acc_scr[dv:dv + 1, 0:tq]
    o1 = acc_scr[0:dv, tq:] / acc_scr[dv:dv + 1, tq:]
    out_t = o0 - lam * o1
    ms = jnp.mean(out_t * out_t, axis=0, keepdims=True)
    hn = (out_t * lax.rsqrt(ms + EPS)).T
    hn = hn * nw_ref[...] * (1.0 - lambda_init)
    o_ref[...] = (hn * _silu(z_ref[...].astype(F32))).astype(BF16)


def _attn(lam_vec, qt, kr, vt, p, a_norm_w, *, batch, seq, lambda_init, tq=1024):
    nq = seq // tq
    tk = tq // 2
    z_col0 = 3584 // A_VDIM
    return pl.pallas_call(
        functools.partial(_attn_kernel, tq=tq, tk=tk, lambda_init=lambda_init),
        grid=(batch, A_HEADS, nq),
        in_specs=[
            pl.BlockSpec((4, A_QK_DIM), lambda b, h, i: (0, 0)),
            pl.BlockSpec((A_VDIM, tq), lambda b, h, i: (b * A_HEADS + h, i)),
            pl.BlockSpec((seq, A_VDIM), lambda b, h, i: (b, h)),
            pl.BlockSpec((A_VDIM, seq), lambda b, h, i: (b * A_HEADS + h, 0)),
            pl.BlockSpec((tq, A_VDIM), lambda b, h, i: (b * nq + i, z_col0 + h)),
            pl.BlockSpec((1, A_VDIM), lambda b, h, i: (0, 0)),
        ],
        out_specs=pl.BlockSpec((tq, A_VDIM), lambda b, h, i: (b * nq + i, h)),
        out_shape=jax.ShapeDtypeStruct((batch * seq, A_WIDTH), BF16),
        scratch_shapes=[
            pltpu.VMEM((1, 2 * tq), F32),
            pltpu.VMEM((A_VDIM + ONES_ROWS, 2 * tq), F32),
            pltpu.VMEM((tk, 2 * tq), F32),
            pltpu.VMEM((tk, 2 * tq), F32),
            pltpu.VMEM((1, 2 * tq), F32),
            pltpu.VMEM((1, 2 * tq), F32),
        ],
        compiler_params=pltpu.CompilerParams(
            dimension_semantics=("arbitrary", "arbitrary", "arbitrary"),
            vmem_limit_bytes=VMEM_LIMIT),
        name="diffattn",
    )(lam_vec, qt, kr, vt, p, a_norm_w.reshape(1, A_VDIM))


def _outproj_kernel(ym_ref, ya_ref, w_ref, x_ref, fw_ref, o_ref, *, final):
    yc = jnp.concatenate([ym_ref[...], ya_ref[...]], axis=1)
    xn = x_ref[...] + jnp.dot(yc, w_ref[...], preferred_element_type=F32)
    if final:
        ms = jnp.mean(xn * xn, axis=-1, keepdims=True)
        xn = xn * lax.rsqrt(ms + EPS) * fw_ref[...]
    o_ref[...] = xn


def _outproj(ym, ya, w_out, x2, final_w, *, final, tm=1024):
    rows = x2.shape[0]
    return pl.pallas_call(
        functools.partial(_outproj_kernel, final=final),
        grid=(rows // tm,),
        in_specs=[
            pl.BlockSpec((tm, M_WIDTH), lambda i: (i, 0)),
            pl.BlockSpec((tm, A_WIDTH), lambda i: (i, 0)),
            pl.BlockSpec((D_MODEL, D_MODEL), lambda i: (0, 0)),
            pl.BlockSpec((tm, D_MODEL), lambda i: (i, 0)),
            pl.BlockSpec((1, D_MODEL), lambda i: (0, 0)),
        ],
        out_specs=pl.BlockSpec((tm, D_MODEL), lambda i: (i, 0)),
        out_shape=jax.ShapeDtypeStruct((rows, D_MODEL), F32),
        compiler_params=pltpu.CompilerParams(
            dimension_semantics=("arbitrary",), vmem_limit_bytes=VMEM_LIMIT),
        name="outproj",
    )(ym, ya, w_out, x2, final_w.reshape(1, D_MODEL))


def _rope_tables(seq):
    dh = A_QK_DIM
    inv = 1.0 / (ROPE_THETA ** (jnp.arange(0, dh, 2, dtype=F32) / dh))
    ang = jnp.arange(seq, dtype=F32)[:, None] * inv[None, :]
    cos = jnp.concatenate([jnp.cos(ang)] * 4, axis=-1)
    sin = jnp.sin(ang)
    sin = jnp.concatenate([-sin, sin, -sin, sin], axis=-1)
    return cos, sin


def kernel(x, norm_w, w_in, conv_w, conv_b, i_bias, f_bias, m_norm_w, m_skip,
           lam_q1, lam_k1, lam_q2, lam_k2, a_norm_w, w_out, final_norm_w):
    batch, seq, _ = x.shape
    depth = w_in.shape[0]
    x2 = x.reshape(batch * seq, D_MODEL)
    cos_t, sin_t = _rope_tables(seq)
    g0 = 3 * M_WIDTH
    g1 = g0 + 2 * M_HEADS
    for l in range(depth):
        lambda_init = 0.8 - 0.6 * math.exp(-0.3 * l)
        w_main = jnp.concatenate([w_in[l, :, :g0], w_in[l, :, g1:]], axis=1).astype(BF16)
        w_gate = jnp.pad(w_in[l, :, g0:g1], ((0, 0), (0, GATE_LANES - 2 * M_HEADS))).astype(BF16)
        gate_bias = jnp.pad(jnp.concatenate([i_bias[l], f_bias[l]]),
                            (0, GATE_LANES - 2 * M_HEADS)).reshape(1, GATE_LANES)
        lam_vec = jnp.stack([lam_q1[l], lam_k1[l], lam_q2[l], lam_k2[l]]).astype(F32)

        p, gates = _inproj(x2, norm_w[l], w_main, w_gate)
        ym = _mlstm(p, gates, conv_w[l], conv_b[l], gate_bias, m_norm_w[l], m_skip[l],
                    batch=batch, seq=seq)
        qt, kr, vt = _rope(p, cos_t, sin_t, batch=batch, seq=seq)
        ya = _attn(lam_vec, qt, kr, vt, p, a_norm_w[l], batch=batch, seq=seq,
                   lambda_init=lambda_init)
        x2 = _outproj(ym, ya, w_out[l].astype(BF16), x2, final_norm_w,
                      final=(l == depth - 1))
    return x2.reshape(batch, seq, D_MODEL)
```

```python
import functools
import math

import jax
import jax.numpy as jnp
from jax import lax
from jax.experimental import pallas as pl
from jax.experimental.pallas import tpu as pltpu

F32 = jnp.float32
BF16 = jnp.bfloat16

D_MODEL = 1024
M_WIDTH = 512
M_HEADS = 4
M_HEAD_DIM = 128
CONV_WIDTH = 4
A_WIDTH = 512
A_HEADS = 4
A_VDIM = 128
A_QK_DIM = 64
ATTN_CHUNK = 64
ROPE_THETA = 10000.0
EPS = 1e-6
LOG2E = 1.4426950408889634

P_WIDTH = 4096
GATE_LANES = 128

VMEM_LIMIT = 56 * 1024 * 1024
NEG_BIG = -1e30


def _silu(y):
    hy = 0.5 * y
    return hy + hy * jnp.tanh(hy)


def _inproj_kernel(x_ref, nw_ref, w_ref, wg_ref, p_ref, g_ref, h_scr):
    @pl.when(pl.program_id(1) == 0)
    def _():
        x = x_ref[...]
        ms = jnp.mean(x * x, axis=-1, keepdims=True)
        h = (x * lax.rsqrt(ms + EPS) * nw_ref[...]).astype(BF16)
        h_scr[...] = h
        g_ref[...] = jnp.dot(h, wg_ref[...], preferred_element_type=F32)

    p_ref[...] = jnp.dot(h_scr[...], w_ref[...], preferred_element_type=F32).astype(BF16)


def _inproj(x2, norm_w, w_main, w_gate, *, tm=1024, tn=2048):
    rows = x2.shape[0]
    return pl.pallas_call(
        _inproj_kernel,
        grid=(rows // tm, P_WIDTH // tn),
        in_specs=[
            pl.BlockSpec((tm, D_MODEL), lambda i, j: (i, 0)),
            pl.BlockSpec((1, D_MODEL), lambda i, j: (0, 0)),
            pl.BlockSpec((D_MODEL, tn), lambda i, j: (0, j)),
            pl.BlockSpec((D_MODEL, GATE_LANES), lambda i, j: (0, 0)),
        ],
        out_specs=[
            pl.BlockSpec((tm, tn), lambda i, j: (i, j)),
            pl.BlockSpec((tm, GATE_LANES), lambda i, j: (i, 0)),
        ],
        out_shape=[
            jax.ShapeDtypeStruct((rows, P_WIDTH), BF16),
            jax.ShapeDtypeStruct((rows, GATE_LANES), F32),
        ],
        scratch_shapes=[pltpu.VMEM((tm, D_MODEL), BF16)],
        compiler_params=pltpu.CompilerParams(
            dimension_semantics=("arbitrary", "arbitrary"),
            vmem_limit_bytes=VMEM_LIMIT),
        name="inproj",
    )(x2, norm_w.reshape(1, D_MODEL), w_main, w_gate)


def _dot_f32_exact_lhs01(tri, x):
    hi = x.astype(BF16)
    r1 = x - hi.astype(F32)
    mid = r1.astype(BF16)
    lo = (r1 - mid.astype(F32)).astype(BF16)
    d = lambda t: jnp.dot(tri, t, preferred_element_type=F32)
    return d(hi) + d(mid) + d(lo)


def _mlstm_kernel(qk_ref, v_ref, z_ref, g_ref, cw_ref, cb_ref, gb_ref, nw_ref, sk_ref,
                  o_ref, xbuf, c_scr, m_scr, *, L):
    n = pl.program_id(1)
    D = M_HEAD_DIM

    @pl.when(n == 0)
    def _():
        xbuf[0:8, :] = jnp.zeros((8, 2 * M_WIDTH), F32)
        c_scr[...] = jnp.zeros(c_scr.shape, F32)
        m_scr[...] = jnp.zeros(m_scr.shape, F32)

    xbuf[8:8 + L, :] = qk_ref[...].astype(F32)
    y = cb_ref[...] + cw_ref[3:4, :] * xbuf[8:8 + L, :]
    y = y + cw_ref[2:3, :] * xbuf[7:7 + L, :]
    y = y + cw_ref[1:2, :] * xbuf[6:6 + L, :]
    y = y + cw_ref[0:1, :] * xbuf[5:5 + L, :]
    xbuf[0:8, :] = xbuf[L:L + 8, :]
    qkc = _silu(y)

    g = g_ref[...] + gb_ref[...]
    lf = jnp.minimum(g, 0.0) - jnp.log1p(jnp.exp(-jnp.abs(g)))
    row = lax.broadcasted_iota(jnp.int32, (L, L), 0)
    col = lax.broadcasted_iota(jnp.int32, (L, L), 1)
    causal = col <= row
    tri = jnp.where(causal, 1.0, 0.0).astype(BF16)
    bcum = _dot_f32_exact_lhs01(tri, lf)
    lane = lax.broadcasted_iota(jnp.int32, (L, GATE_LANES), 1)
    a_col = jnp.where(lane < M_HEADS, g, bcum)
    a_row = a_col.T

    ones = jnp.ones((L, D), BF16)
    for h in range(M_HEADS):
        sl = slice(h * D, (h + 1) * D)
        b_c = a_col[:, M_HEADS + h:M_HEADS + h + 1]
        li_c = a_col[:, h:h + 1]
        b_r = a_row[M_HEADS + h:M_HEADS + h + 1, :]
        li_r = a_row[h:h + 1, :]
        m_prev = m_scr[h:h + 1, 0:1]

        dm = jnp.where(causal, b_c - b_r + li_r, NEG_BIG)
        inter = b_c + m_prev
        m_i = jnp.maximum(inter, jnp.max(dm, axis=1, keepdims=True))
        w_intra = jnp.exp(dm - m_i)
        w_inter = jnp.exp(inter - m_i)

        qh = qkc[:, sl]
        kh = qkc[:, M_WIDTH + h * D:M_WIDTH + (h + 1) * D] * (D ** -0.5)
        qb = qh.astype(BF16)
        v_ext = jnp.concatenate([v_ref[:, sl], ones], axis=1)
        s = lax.dot_general(qb, kh.astype(BF16), (((1,), (1,)), ((), ())),
                            preferred_element_type=F32) * w_intra
        c_old = c_scr[h]
        q_c = jnp.dot(qb, c_old.astype(BF16), preferred_element_type=F32)
        s_v = jnp.dot(s.astype(BF16), v_ext, preferred_element_type=F32)
        num = w_inter * q_c[:, :D] + s_v[:, :D]
        den = w_inter * q_c[:, D:D + 1] + s_v[:, D:D + 1]
        hm = num / jnp.maximum(jnp.abs(den), jnp.exp(-m_i))

        b_last = b_c[L - 1:L, :]
        gg = b_last - b_c + li_c
        m_new = jnp.maximum(b_last + m_prev, jnp.max(gg, axis=0, keepdims=True))
        decay = jnp.exp(b_last + m_prev - m_new)
        wk = jnp.exp(gg - m_new)
        kw_t = (kh * wk).T.astype(BF16)
        c_scr[h] = decay * c_old + jnp.dot(kw_t, v_ext, preferred_element_type=F32)
        m_scr[h:h + 1, :] = jnp.broadcast_to(m_new, (1, m_scr.shape[1]))

        ms = jnp.mean(hm * hm, axis=-1, keepdims=True)
        hn = hm * lax.rsqrt(ms + EPS) * nw_ref[:, sl]
        ym = (hn + sk_ref[:, sl] * qh) * _silu(z_ref[:, sl].astype(F32))
        o_ref[:, sl] = ym.astype(BF16)


def _mlstm(p, gates, conv_w, conv_b, gate_bias, m_norm_w, m_skip, *, batch, seq, L=256):
    nchunk = seq // L
    rb = lambda b, n: b * nchunk + n
    return pl.pallas_call(
        functools.partial(_mlstm_kernel, L=L),
        grid=(batch, nchunk),
        in_specs=[
            pl.BlockSpec((L, 2 * M_WIDTH), lambda b, n: (rb(b, n), 0)),
            pl.BlockSpec((L, M_WIDTH), lambda b, n: (rb(b, n), 2)),
            pl.BlockSpec((L, M_WIDTH), lambda b, n: (rb(b, n), 3)),
            pl.BlockSpec((L, GATE_LANES), lambda b, n: (rb(b, n), 0)),
            pl.BlockSpec((CONV_WIDTH, 2 * M_WIDTH), lambda b, n: (0, 0)),
            pl.BlockSpec((1, 2 * M_WIDTH), lambda b, n: (0, 0)),
            pl.BlockSpec((1, GATE_LANES), lambda b, n: (0, 0)),
            pl.BlockSpec((1, M_WIDTH), lambda b, n: (0, 0)),
            pl.BlockSpec((1, M_WIDTH), lambda b, n: (0, 0)),
        ],
        out_specs=pl.BlockSpec((L, M_WIDTH), lambda b, n: (rb(b, n), 0)),
        out_shape=jax.ShapeDtypeStruct((batch * seq, M_WIDTH), BF16),
        scratch_shapes=[
            pltpu.VMEM((L + 8, 2 * M_WIDTH), F32),
            pltpu.VMEM((M_HEADS, M_HEAD_DIM, 2 * M_HEAD_DIM), F32),
            pltpu.VMEM((8, 128), F32),
        ],
        compiler_params=pltpu.CompilerParams(
            dimension_semantics=("arbitrary", "arbitrary"),
            vmem_limit_bytes=VMEM_LIMIT),
        name="mlstm",
    )(p, p, p, gates, conv_w, conv_b.reshape(1, -1), gate_bias,
      m_norm_w.reshape(1, -1), m_skip.reshape(1, -1))


def _rope_kernel(q_ref, k_ref, v_ref, cos_ref, sin_ref, qt_ref, ko_ref, vt_ref, *, q_scale):
    cos = jnp.concatenate([cos_ref[...]] * A_HEADS, axis=1)
    sin = jnp.concatenate([sin_ref[...]] * A_HEADS, axis=1)
    lane = lax.broadcasted_iota(jnp.int32, cos.shape, 1)
    first_half = (lane & (A_QK_DIM - 1)) < (A_QK_DIM // 2)

    def rot(x):
        fwd = pltpu.roll(x, A_QK_DIM // 2, 1)
        bwd = pltpu.roll(x, A_WIDTH - A_QK_DIM // 2, 1)
        return x * cos + jnp.where(first_half, bwd, fwd) * sin

    qt_ref[...] = (rot(q_ref[...].astype(F32)) * q_scale).T.astype(BF16)
    ko_ref[...] = rot(k_ref[...].astype(F32)).astype(BF16)
    vt_ref[...] = v_ref[...].astype(F32).T.astype(BF16)


def _rope(p, cos_t, sin_t, *, batch, seq, tm=1024):
    rows = p.shape[0]
    nsb = seq // tm
    q_scale = (A_QK_DIM ** -0.5) * LOG2E
    nat = jax.ShapeDtypeStruct((rows, A_WIDTH), BF16)
    tr = jax.ShapeDtypeStruct((batch * A_WIDTH, seq), BF16)
    tr_spec = pl.BlockSpec((A_WIDTH, tm), lambda i: (i // nsb, i % nsb))
    return pl.pallas_call(
        functools.partial(_rope_kernel, q_scale=q_scale),
        grid=(rows // tm,),
        in_specs=[
            pl.BlockSpec((tm, A_WIDTH), lambda i: (i, 4)),
            pl.BlockSpec((tm, A_WIDTH), lambda i: (i, 5)),
            pl.BlockSpec((tm, A_WIDTH), lambda i: (i, 6)),
            pl.BlockSpec((tm, 2 * A_QK_DIM), lambda i: (i % nsb, 0)),
            pl.BlockSpec((tm, 2 * A_QK_DIM), lambda i: (i % nsb, 0)),
        ],
        out_specs=[tr_spec, pl.BlockSpec((tm, A_WIDTH), lambda i: (i, 0)), tr_spec],
        out_shape=[tr, nat, tr],
        compiler_params=pltpu.CompilerParams(
            dimension_semantics=("arbitrary",), vmem_limit_bytes=VMEM_LIMIT),
        name="rope",
    )(p, p, p, cos_t, sin_t)


ONES_ROWS = 16
QGROUP = 256


def _attn_kernel(lam_ref, qt_ref, k_ref, vt_ref, z_ref, nw_ref, o_ref,
                 m_scr, acc_scr, q2_scr, sa, sb, cma, cmb, *, tq, tk, nq, lambda_init):
    dv = A_VDIM
    ng = 2 * tq // QGROUP
    ones = jnp.ones((ONES_ROWS, tk), BF16)

    def load_queries(i):
        qt = qt_ref[:, pl.ds(pl.multiple_of(i * tq, tq), tq)]
        rowi = lax.broadcasted_iota(jnp.int32, qt.shape, 0)
        zero = jnp.zeros_like(qt)
        q2_scr[:, 0:tq] = jnp.where(rowi < A_QK_DIM, qt, zero)
        q2_scr[:, tq:] = jnp.where(rowi >= A_QK_DIM, qt, zero)

    def reset_stats():
        m_scr[...] = jnp.full(m_scr.shape, NEG_BIG, F32)
        acc_scr[...] = jnp.zeros(acc_scr.shape, F32)

    def all_masked(diag_off, g):
        return diag_off is not None and diag_off >= (g * QGROUP) % tq + QGROUP

    def scores_group(start, s_ref, cm_ref, diag_off, g):
        gs = slice(g * QGROUP, (g + 1) * QGROUP)
        qlo = (g * QGROUP) % tq
        if all_masked(diag_off, g):
            return
        k = k_ref[pl.ds(start, tk), :]
        st = jnp.dot(k, q2_scr[:, gs], preferred_element_type=F32)
        if diag_off is not None and diag_off + tk > qlo + ATTN_CHUNK:
            kk = lax.broadcasted_iota(jnp.int32, st.shape, 0) + diag_off
            qq = lax.broadcasted_iota(jnp.int32, st.shape, 1) + qlo
            st = jnp.where((kk // ATTN_CHUNK) <= (qq // ATTN_CHUNK), st, NEG_BIG)
        s_ref[:, gs] = st
        cm_ref[:, gs] = jnp.max(st, axis=0, keepdims=True)

    def consume_group(start, s_ref, cm_ref, diag_off, g):
        gs = slice(g * QGROUP, (g + 1) * QGROUP)
        if all_masked(diag_off, g):
            return
        vt_ext = jnp.concatenate([vt_ref[:, pl.ds(start, tk)], ones], axis=0)
        m_old = m_scr[:, gs]
        m_new = jnp.maximum(m_old, cm_ref[:, gs])
        alpha = jnp.exp2(m_old - m_new)
        pt = jnp.exp2((s_ref[:, gs] - m_new).astype(BF16))
        acc_scr[:, gs] = alpha * acc_scr[:, gs] + jnp.dot(vt_ext, pt,
                                                          preferred_element_type=F32)
        m_scr[:, gs] = m_new

    def stage(score_args, consume_args):
        for g in range(ng):
            if score_args is not None:
                scores_group(*score_args, g)
            if consume_args is not None:
                consume_group(*consume_args, g)

    lv = lam_ref[...]
    lam = (jnp.exp(jnp.sum(lv[0:1, :] * lv[1:2, :], axis=1, keepdims=True))
           - jnp.exp(jnp.sum(lv[2:3, :] * lv[3:4, :], axis=1, keepdims=True))
           + lambda_init)

    def finalize(i):
        rows = pl.ds(pl.multiple_of(i * tq, tq), tq)
        o0 = acc_scr[0:dv, 0:tq] / acc_scr[dv:dv + 1, 0:tq]
        o1 = acc_scr[0:dv, tq:] / acc_scr[dv:dv + 1, tq:]
        out_t = o0 - lam * o1
        ms = jnp.mean(out_t * out_t, axis=0, keepdims=True)
        hn = (out_t * lax.rsqrt(ms + EPS)).T
        hn = hn * nw_ref[...] * (1.0 - lambda_init)
        o_ref[rows, :] = (hn * _silu(z_ref[rows, :].astype(F32))).astype(BF16)

    def off(x):
        return pl.multiple_of(x, tk)

    def diag_tail(i):
        d0 = i * tq
        stage((off(d0 + tk), sb, cmb, tk), (off(d0), sa, cma, 0))
        load_queries(jnp.minimum(i + 1, nq - 1))
        stage((0, sa, cma, None), (off(d0 + tk), sb, cmb, tk))
        finalize(i)
        reset_stats()

    reset_stats()
    load_queries(0)
    stage((0, sa, cma, 0), None)
    diag_tail(0)

    def q_block(i, carry):
        def body(t, c):
            u = 2 * t * tk
            stage((off(u + tk), sb, cmb, None), (off(u), sa, cma, None))
            stage((off(u + 2 * tk), sa, cma, None), (off(u + tk), sb, cmb, None))
            return c

        lax.fori_loop(0, i - 1, body, 0)
        u = (2 * i - 2) * tk
        stage((off(u + tk), sb, cmb, None), (off(u), sa, cma, None))
        stage((off(i * tq), sa, cma, 0), (off(u + tk), sb, cmb, None))
        diag_tail(i)
        return carry

    lax.fori_loop(1, nq, q_block, 0)


def _attn(lam_vec, qt, kr, vt, p, a_norm_w, *, batch, seq, lambda_init, tq=1024):
    nq = seq // tq
    tk = tq // 2
    z_col0 = 3584 // A_VDIM
    return pl.pallas_call(
        functools.partial(_attn_kernel, tq=tq, tk=tk, nq=nq, lambda_init=lambda_init),
        grid=(batch, A_HEADS),
        in_specs=[
            pl.BlockSpec((4, A_QK_DIM), lambda b, h: (0, 0)),
            pl.BlockSpec((A_VDIM, seq), lambda b, h: (b * A_HEADS + h, 0)),
            pl.BlockSpec((seq, A_VDIM), lambda b, h: (b, h)),
            pl.BlockSpec((A_VDIM, seq), lambda b, h: (b * A_HEADS + h, 0)),
            pl.BlockSpec((seq, A_VDIM), lambda b, h: (b, z_col0 + h)),
            pl.BlockSpec((1, A_VDIM), lambda b, h: (0, 0)),
        ],
        out_specs=pl.BlockSpec((seq, A_VDIM), lambda b, h: (b, h)),
        out_shape=jax.ShapeDtypeStruct((batch * seq, A_WIDTH), BF16),
        scratch_shapes=[
            pltpu.VMEM((1, 2 * tq), F32),
            pltpu.VMEM((A_VDIM + ONES_ROWS, 2 * tq), F32),
            pltpu.VMEM((A_VDIM, 2 * tq), BF16),
            pltpu.VMEM((tk, 2 * tq), F32),
            pltpu.VMEM((tk, 2 * tq), F32),
            pltpu.VMEM((1, 2 * tq), F32),
            pltpu.VMEM((1, 2 * tq), F32),
        ],
        compiler_params=pltpu.CompilerParams(
            dimension_semantics=("arbitrary", "arbitrary"),
            vmem_limit_bytes=VMEM_LIMIT),
        name="diffattn",
    )(lam_vec, qt, kr, vt, p, a_norm_w.reshape(1, A_VDIM))


def _outproj_kernel(ym_ref, ya_ref, w_ref, x_ref, fw_ref, o_ref, *, final):
    yc = jnp.concatenate([ym_ref[...], ya_ref[...]], axis=1)
    xn = x_ref[...] + jnp.dot(yc, w_ref[...], preferred_element_type=F32)
    if final:
        ms = jnp.mean(xn * xn, axis=-1, keepdims=True)
        xn = xn * lax.rsqrt(ms + EPS) * fw_ref[...]
    o_ref[...] = xn


def _outproj(ym, ya, w_out, x2, final_w, *, final, tm=1024):
    rows = x2.shape[0]
    return pl.pallas_call(
        functools.partial(_outproj_kernel, final=final),
        grid=(rows // tm,),
        in_specs=[
            pl.BlockSpec((tm, M_WIDTH), lambda i: (i, 0)),
            pl.BlockSpec((tm, A_WIDTH), lambda i: (i, 0)),
            pl.BlockSpec((D_MODEL, D_MODEL), lambda i: (0, 0)),
            pl.BlockSpec((tm, D_MODEL), lambda i: (i, 0)),
            pl.BlockSpec((1, D_MODEL), lambda i: (0, 0)),
        ],
        out_specs=pl.BlockSpec((tm, D_MODEL), lambda i: (i, 0)),
        out_shape=jax.ShapeDtypeStruct((rows, D_MODEL), F32),
        compiler_params=pltpu.CompilerParams(
            dimension_semantics=("arbitrary",), vmem_limit_bytes=VMEM_LIMIT),
        name="outproj",
    )(ym, ya, w_out, x2, final_w.reshape(1, D_MODEL))


def _rope_tables(seq):
    dh = A_QK_DIM
    inv = 1.0 / (ROPE_THETA ** (jnp.arange(0, dh, 2, dtype=F32) / dh))
    ang = jnp.arange(seq, dtype=F32)[:, None] * inv[None, :]
    cos = jnp.concatenate([jnp.cos(ang)] * 4, axis=-1)
    sin = jnp.sin(ang)
    sin = jnp.concatenate([-sin, sin, -sin, sin], axis=-1)
    return cos, sin


def kernel(x, norm_w, w_in, conv_w, conv_b, i_bias, f_bias, m_norm_w, m_skip,
           lam_q1, lam_k1, lam_q2, lam_k2, a_norm_w, w_out, final_norm_w):
    batch, seq, _ = x.shape
    depth = w_in.shape[0]
    x2 = x.reshape(batch * seq, D_MODEL)
    cos_t, sin_t = _rope_tables(seq)
    g0 = 3 * M_WIDTH
    g1 = g0 + 2 * M_HEADS
    for l in range(depth):
        lambda_init = 0.8 - 0.6 * math.exp(-0.3 * l)
        w_main = jnp.concatenate([w_in[l, :, :g0], w_in[l, :, g1:]], axis=1).astype(BF16)
        w_gate = jnp.pad(w_in[l, :, g0:g1], ((0, 0), (0, GATE_LANES - 2 * M_HEADS))).astype(BF16)
        gate_bias = jnp.pad(jnp.concatenate([i_bias[l], f_bias[l]]),
                            (0, GATE_LANES - 2 * M_HEADS)).reshape(1, GATE_LANES)
        lam_vec = jnp.stack([lam_q1[l], lam_k1[l], lam_q2[l], lam_k2[l]]).astype(F32)

        p, gates = _inproj(x2, norm_w[l], w_main, w_gate)
        ym = _mlstm(p, gates, conv_w[l], conv_b[l], gate_bias, m_norm_w[l], m_skip[l],
                    batch=batch, seq=seq)
        qt, kr, vt = _rope(p, cos_t, sin_t, batch=batch, seq=seq)
        ya = _attn(lam_vec, qt, kr, vt, p, a_norm_w[l], batch=batch, seq=seq,
                   lambda_init=lambda_init)
        x2 = _outproj(ym, ya, w_out[l].astype(BF16), x2, final_norm_w,
                      final=(l == depth - 1))
    return x2.reshape(batch, seq, D_MODEL)
```

```python
import functools
import math

import jax
import jax.numpy as jnp
from jax import lax
from jax.experimental import pallas as pl
from jax.experimental.pallas import tpu as pltpu

F32 = jnp.float32
BF16 = jnp.bfloat16

D_MODEL = 1024
M_WIDTH = 512
M_HEADS = 4
M_HEAD_DIM = 128
CONV_WIDTH = 4
A_WIDTH = 512
A_HEADS = 4
A_VDIM = 128
A_QK_DIM = 64
ATTN_CHUNK = 64
ROPE_THETA = 10000.0
EPS = 1e-6
LOG2E = 1.4426950408889634

P_WIDTH = 4096
GATE_LANES = 128
STAT_ROWS = 8

VMEM_LIMIT = 56 * 1024 * 1024
NEG_BIG = -1e30


def _silu(y):
    hy = 0.5 * y
    return hy + hy * jnp.tanh(hy)


def _inproj_kernel(x_ref, nw_ref, w_ref, wg_ref, p_ref, g_ref, h_scr):
    @pl.when(pl.program_id(1) == 0)
    def _():
        x = x_ref[...]
        ms = jnp.mean(x * x, axis=-1, keepdims=True)
        h = (x * lax.rsqrt(ms + EPS) * nw_ref[...]).astype(BF16)
        h_scr[...] = h
        g_ref[...] = jnp.dot(h, wg_ref[...], preferred_element_type=F32)

    p_ref[...] = jnp.dot(h_scr[...], w_ref[...], preferred_element_type=F32).astype(BF16)


def _inproj(x2, norm_w, w_main, w_gate, *, tm=1024, tn=2048):
    rows = x2.shape[0]
    return pl.pallas_call(
        _inproj_kernel,
        grid=(rows // tm, P_WIDTH // tn),
        in_specs=[
            pl.BlockSpec((tm, D_MODEL), lambda i, j: (i, 0)),
            pl.BlockSpec((1, D_MODEL), lambda i, j: (0, 0)),
            pl.BlockSpec((D_MODEL, tn), lambda i, j: (0, j)),
            pl.BlockSpec((D_MODEL, GATE_LANES), lambda i, j: (0, 0)),
        ],
        out_specs=[
            pl.BlockSpec((tm, tn), lambda i, j: (i, j)),
            pl.BlockSpec((tm, GATE_LANES), lambda i, j: (i, 0)),
        ],
        out_shape=[
            jax.ShapeDtypeStruct((rows, P_WIDTH), BF16),
            jax.ShapeDtypeStruct((rows, GATE_LANES), F32),
        ],
        scratch_shapes=[pltpu.VMEM((tm, D_MODEL), BF16)],
        compiler_params=pltpu.CompilerParams(
            dimension_semantics=("arbitrary", "arbitrary"),
            vmem_limit_bytes=VMEM_LIMIT),
        name="inproj",
    )(x2, norm_w.reshape(1, D_MODEL), w_main, w_gate)


def _dot_f32_exact_rhs01(x, tri):
    hi = x.astype(BF16)
    r1 = x - hi.astype(F32)
    mid = r1.astype(BF16)
    lo = (r1 - mid.astype(F32)).astype(BF16)
    d = lambda t: jnp.dot(t, tri, preferred_element_type=F32)
    return d(hi) + d(mid) + d(lo)


def _mlstm_kernel(qk_ref, v_ref, z_ref, g_ref, cw_ref, cb_ref, gb_ref, nw_ref, sk_ref,
                  o_ref, xbuf, qkc, hm_scr, c_scr, m_scr, *, L, batch):
    n = pl.program_id(0)
    D = M_HEAD_DIM
    R = STAT_ROWS
    bs = range(batch)

    @pl.when(n == 0)
    def _():
        xbuf[:, 0:8, :] = jnp.zeros((batch, 8, 2 * M_WIDTH), F32)
        c_scr[...] = jnp.zeros(c_scr.shape, F32)
        m_scr[...] = jnp.zeros(m_scr.shape, F32)

    row = lax.broadcasted_iota(jnp.int32, (L, L), 0)
    col = lax.broadcasted_iota(jnp.int32, (L, L), 1)
    causal = col <= row
    triu = jnp.where(row <= col, 1.0, 0.0).astype(BF16)
    lane = lax.broadcasted_iota(jnp.int32, (R, L), 1)

    def gate_stats(bi):
        gt = (g_ref[bi] + gb_ref[...]).T
        li = gt[0:R, :]
        fp = gt[R:2 * R, :]
        lf = jnp.minimum(fp, 0.0) - jnp.log1p(jnp.exp(-jnp.abs(fp)))
        b = _dot_f32_exact_rhs01(lf, triu)
        a = li - b
        pm = a
        shift = 1
        while shift < L:
            pm = jnp.maximum(pm, jnp.where(lane >= shift, pltpu.roll(pm, shift, 1), NEG_BIG))
            shift *= 2
        m_old = m_scr[bi]
        m_prev = jnp.concatenate([m_old] * (L // m_old.shape[1]), axis=1)
        m_i = jnp.maximum(b + m_prev, b + pm)
        w_inter = jnp.exp(b + m_prev - m_i)
        e_neg = jnp.exp(-m_i)
        b_last = b[:, L - 1:L]
        m_new = b_last + jnp.maximum(m_old[:, 0:1], pm[:, L - 1:L])
        decay = jnp.exp(b_last + m_old[:, 0:1] - m_new)
        wk = jnp.exp(b_last + a - m_new)
        m_scr[bi] = jnp.broadcast_to(m_new, m_old.shape)
        cols = jnp.concatenate([(m_i - b) * LOG2E, w_inter, e_neg, wk,
                                jnp.zeros((GATE_LANES - 4 * R, L), F32)], axis=0).T
        return a * LOG2E, cols, decay

    stats = []
    for bi in bs:
        stats.append(gate_stats(bi))
        xb = xbuf.at[bi]
        xb[8:8 + L, :] = qk_ref[bi].astype(F32)
        y = cb_ref[...] + cw_ref[3:4, :] * xb[8:8 + L, :]
        y = y + cw_ref[2:3, :] * xb[7:7 + L, :]
        y = y + cw_ref[1:2, :] * xb[6:6 + L, :]
        y = y + cw_ref[0:1, :] * xb[5:5 + L, :]
        xb[0:8, :] = xb[L:L + 8, :]
        qkc[bi] = _silu(y)

    ones = jnp.ones((L, D), BF16)
    for h in range(M_HEADS):
        sl = slice(h * D, (h + 1) * D)
        for bi in bs:
            a2, cols, decay = stats[bi]
            c_col = cols[:, h:h + 1]
            wi_col = cols[:, R + h:R + h + 1]
            en_col = cols[:, 2 * R + h:2 * R + h + 1]
            wk_col = cols[:, 3 * R + h:3 * R + h + 1]
            w_intra = jnp.where(causal, jnp.exp2(a2[h:h + 1, :] - c_col), 0.0)

            qb = qkc[bi, :, sl].astype(BF16)
            kh = qkc[bi, :, M_WIDTH + h * D:M_WIDTH + (h + 1) * D] * (D ** -0.5)
            v_ext = jnp.concatenate([v_ref[bi, :, sl], ones], axis=1)
            s = lax.dot_general(qb, kh.astype(BF16), (((1,), (1,)), ((), ())),
                                preferred_element_type=F32) * w_intra
            c_old = c_scr[bi, h]
            q_c = jnp.dot(qb, c_old.astype(BF16), preferred_element_type=F32)
            s_v = jnp.dot(s.astype(BF16), v_ext, preferred_element_type=F32)
            num = wi_col * q_c[:, :D] + s_v[:, :D]
            den = wi_col * q_c[:, D:D + 1] + s_v[:, D:D + 1]
            hm_scr[bi, :, sl] = num * (1.0 / jnp.maximum(jnp.abs(den), en_col))

            kw_t = (kh * wk_col).T.astype(BF16)
            c_scr[bi, h] = decay[h:h + 1, :] * c_old + jnp.dot(kw_t, v_ext,
                                                               preferred_element_type=F32)

    for h in range(M_HEADS):
        sl = slice(h * D, (h + 1) * D)
        for bi in bs:
            hm = hm_scr[bi, :, sl]
            ms = jnp.mean(hm * hm, axis=-1, keepdims=True)
            hn = hm * lax.rsqrt(ms + EPS) * nw_ref[:, sl]
            ym = (hn + sk_ref[:, sl] * qkc[bi, :, sl]) * _silu(z_ref[bi, :, sl].astype(F32))
            o_ref[bi, :, sl] = ym.astype(BF16)


def _mlstm(p, gates, conv_w, conv_b, gate_bias, m_norm_w, m_skip, *, batch, seq, L=256):
    nchunk = seq // L
    p3 = p.reshape(batch, seq, P_WIDTH)
    g3 = gates.reshape(batch, seq, GATE_LANES)
    const = lambda n: (0, 0)
    ym = pl.pallas_call(
        functools.partial(_mlstm_kernel, L=L, batch=batch),
        grid=(nchunk,),
        in_specs=[
            pl.BlockSpec((batch, L, 2 * M_WIDTH), lambda n: (0, n, 0)),
            pl.BlockSpec((batch, L, M_WIDTH), lambda n: (0, n, 2)),
            pl.BlockSpec((batch, L, M_WIDTH), lambda n: (0, n, 3)),
            pl.BlockSpec((batch, L, GATE_LANES), lambda n: (0, n, 0)),
            pl.BlockSpec((CONV_WIDTH, 2 * M_WIDTH), const),
            pl.BlockSpec((1, 2 * M_WIDTH), const),
            pl.BlockSpec((1, GATE_LANES), const),
            pl.BlockSpec((1, M_WIDTH), const),
            pl.BlockSpec((1, M_WIDTH), const),
        ],
        out_specs=pl.BlockSpec((batch, L, M_WIDTH), lambda n: (0, n, 0)),
        out_shape=jax.ShapeDtypeStruct((batch, seq, M_WIDTH), BF16),
        scratch_shapes=[
            pltpu.VMEM((batch, L + 8, 2 * M_WIDTH), F32),
            pltpu.VMEM((batch, L, 2 * M_WIDTH), F32),
            pltpu.VMEM((batch, L, M_WIDTH), F32),
            pltpu.VMEM((batch, M_HEADS, M_HEAD_DIM, 2 * M_HEAD_DIM), F32),
            pltpu.VMEM((batch, STAT_ROWS, 128), F32),
        ],
        compiler_params=pltpu.CompilerParams(
            dimension_semantics=("arbitrary",),
            vmem_limit_bytes=VMEM_LIMIT),
        name="mlstm",
    )(p3, p3, p3, g3, conv_w, conv_b.reshape(1, -1), gate_bias,
      m_norm_w.reshape(1, -1), m_skip.reshape(1, -1))
    return ym.reshape(batch * seq, M_WIDTH)


def _rope_kernel(q_ref, k_ref, v_ref, cos_ref, sin_ref, qt_ref, ko_ref, vt_ref, *, q_scale):
    cos = jnp.concatenate([cos_ref[...]] * A_HEADS, axis=1)
    sin = jnp.concatenate([sin_ref[...]] * A_HEADS, axis=1)
    lane = lax.broadcasted_iota(jnp.int32, cos.shape, 1)
    first_half = (lane & (A_QK_DIM - 1)) < (A_QK_DIM // 2)

    def rot(x):
        fwd = pltpu.roll(x, A_QK_DIM // 2, 1)
        bwd = pltpu.roll(x, A_WIDTH - A_QK_DIM // 2, 1)
        return x * cos + jnp.where(first_half, bwd, fwd) * sin

    qt_ref[...] = (rot(q_ref[...].astype(F32)) * q_scale).T.astype(BF16)
    ko_ref[...] = rot(k_ref[...].astype(F32)).astype(BF16)
    vt_ref[...] = v_ref[...].astype(F32).T.astype(BF16)


def _rope(p, cos_t, sin_t, *, batch, seq, tm=1024):
    rows = p.shape[0]
    nsb = seq // tm
    q_scale = (A_QK_DIM ** -0.5) * LOG2E
    nat = jax.ShapeDtypeStruct((rows, A_WIDTH), BF16)
    tr = jax.ShapeDtypeStruct((batch * A_WIDTH, seq), BF16)
    tr_spec = pl.BlockSpec((A_WIDTH, tm), lambda i: (i // nsb, i % nsb))
    return pl.pallas_call(
        functools.partial(_rope_kernel, q_scale=q_scale),
        grid=(rows // tm,),
        in_specs=[
            pl.BlockSpec((tm, A_WIDTH), lambda i: (i, 4)),
            pl.BlockSpec((tm, A_WIDTH), lambda i: (i, 5)),
            pl.BlockSpec((tm, A_WIDTH), lambda i: (i, 6)),
            pl.BlockSpec((tm, 2 * A_QK_DIM), lambda i: (i % nsb, 0)),
            pl.BlockSpec((tm, 2 * A_QK_DIM), lambda i: (i % nsb, 0)),
        ],
        out_specs=[tr_spec, pl.BlockSpec((tm, A_WIDTH), lambda i: (i, 0)), tr_spec],
        out_shape=[tr, nat, tr],
        compiler_params=pltpu.CompilerParams(
            dimension_semantics=("arbitrary",), vmem_limit_bytes=VMEM_LIMIT),
        name="rope",
    )(p, p, p, cos_t, sin_t)


ONES_ROWS = 16
QGROUP = 256


def _attn_kernel(lam_ref, qt_ref, k_ref, vt_ref, z_ref, nw_ref, o_ref,
                 m_scr, acc_scr, q2_scr, sa, sb, cma, cmb, *, tq, tk, nq, lambda_init):
    dv = A_VDIM
    ng = 2 * tq // QGROUP
    ones = jnp.ones((ONES_ROWS, tk), BF16)

    def load_queries(i):
        qt = qt_ref[:, pl.ds(pl.multiple_of(i * tq, tq), tq)]
        rowi = lax.broadcasted_iota(jnp.int32, qt.shape, 0)
        zero = jnp.zeros_like(qt)
        q2_scr[:, 0:tq] = jnp.where(rowi < A_QK_DIM, qt, zero)
        q2_scr[:, tq:] = jnp.where(rowi >= A_QK_DIM, qt, zero)

    def reset_stats():
        m_scr[...] = jnp.full(m_scr.shape, NEG_BIG, F32)
        acc_scr[...] = jnp.zeros(acc_scr.shape, F32)

    def all_masked(diag_off, g):
        return diag_off is not None and diag_off >= (g * QGROUP) % tq + QGROUP

    def scores_group(start, s_ref, cm_ref, diag_off, g):
        gs = slice(g * QGROUP, (g + 1) * QGROUP)
        qlo = (g * QGROUP) % tq
        if all_masked(diag_off, g):
            return
        k = k_ref[pl.ds(start, tk), :]
        st = jnp.dot(k, q2_scr[:, gs], preferred_element_type=F32)
        if diag_off is not None and diag_off + tk > qlo + ATTN_CHUNK:
            kk = lax.broadcasted_iota(jnp.int32, st.shape, 0) + diag_off
            qq = lax.broadcasted_iota(jnp.int32, st.shape, 1) + qlo
            st = jnp.where((kk // ATTN_CHUNK) <= (qq // ATTN_CHUNK), st, NEG_BIG)
        s_ref[:, gs] = st
        cm_ref[:, gs] = jnp.max(st, axis=0, keepdims=True)

    def consume_group(start, s_ref, cm_ref, diag_off, g):
        gs = slice(g * QGROUP, (g + 1) * QGROUP)
        if all_masked(diag_off, g):
            return
        vt_ext = jnp.concatenate([vt_ref[:, pl.ds(start, tk)], ones], axis=0)
        m_old = m_scr[:, gs]
        m_new = jnp.maximum(m_old, cm_ref[:, gs])
        alpha = jnp.exp2(m_old - m_new)
        pt = jnp.exp2((s_ref[:, gs] - m_new).astype(BF16))
        acc_scr[:, gs] = alpha * acc_scr[:, gs] + jnp.dot(vt_ext, pt,
                                                          preferred_element_type=F32)
        m_scr[:, gs] = m_new

    def stage(score_args, consume_args):
        for g in range(ng):
            if score_args is not None:
                scores_group(*score_args, g)
            if consume_args is not None:
                consume_group(*consume_args, g)

    lv = lam_ref[...]
    lam = (jnp.exp(jnp.sum(lv[0:1, :] * lv[1:2, :], axis=1, keepdims=True))
           - jnp.exp(jnp.sum(lv[2:3, :] * lv[3:4, :], axis=1, keepdims=True))
           + lambda_init)

    def finalize(i):
        rows = pl.ds(pl.multiple_of(i * tq, tq), tq)
        o0 = acc_scr[0:dv, 0:tq] / acc_scr[dv:dv + 1, 0:tq]
        o1 = acc_scr[0:dv, tq:] / acc_scr[dv:dv + 1, tq:]
        out_t = o0 - lam * o1
        ms = jnp.mean(out_t * out_t, axis=0, keepdims=True)
        hn = (out_t * lax.rsqrt(ms + EPS)).T
        hn = hn * nw_ref[...] * (1.0 - lambda_init)
        o_ref[rows, :] = (hn * _silu(z_ref[rows, :].astype(F32))).astype(BF16)

    def off(x):
        return pl.multiple_of(x, tk)

    def diag_tail(i):
        d0 = i * tq
        stage((off(d0 + tk), sb, cmb, tk), (off(d0), sa, cma, 0))
        load_queries(jnp.minimum(i + 1, nq - 1))
        stage((0, sa, cma, None), (off(d0 + tk), sb, cmb, tk))
        finalize(i)
        reset_stats()

    reset_stats()
    load_queries(0)
    stage((0, sa, cma, 0), None)
    diag_tail(0)

    def q_block(i, carry):
        def body(t, c):
            u = 2 * t * tk
            stage((off(u + tk), sb, cmb, None), (off(u), sa, cma, None))
            stage((off(u + 2 * tk), sa, cma, None), (off(u + tk), sb, cmb, None))
            return c

        lax.fori_loop(0, i - 1, body, 0)
        u = (2 * i - 2) * tk
        stage((off(u + tk), sb, cmb, None), (off(u), sa, cma, None))
        stage((off(i * tq), sa, cma, 0), (off(u + tk), sb, cmb, None))
        diag_tail(i)
        return carry

    lax.fori_loop(1, nq, q_block, 0)


def _attn(lam_vec, qt, kr, vt, p, a_norm_w, *, batch, seq, lambda_init, tq=1024):
    nq = seq // tq
    tk = tq // 2
    z_col0 = 3584 // A_VDIM
    return pl.pallas_call(
        functools.partial(_attn_kernel, tq=tq, tk=tk, nq=nq, lambda_init=lambda_init),
        grid=(batch, A_HEADS),
        in_specs=[
            pl.BlockSpec((4, A_QK_DIM), lambda b, h: (0, 0)),
            pl.BlockSpec((A_VDIM, seq), lambda b, h: (b * A_HEADS + h, 0)),
            pl.BlockSpec((seq, A_VDIM), lambda b, h: (b, h)),
            pl.BlockSpec((A_VDIM, seq), lambda b, h: (b * A_HEADS + h, 0)),
            pl.BlockSpec((seq, A_VDIM), lambda b, h: (b, z_col0 + h)),
            pl.BlockSpec((1, A_VDIM), lambda b, h: (0, 0)),
        ],
        out_specs=pl.BlockSpec((seq, A_VDIM), lambda b, h: (b, h)),
        out_shape=jax.ShapeDtypeStruct((batch * seq, A_WIDTH), BF16),
        scratch_shapes=[
            pltpu.VMEM((1, 2 * tq), F32),
            pltpu.VMEM((A_VDIM + ONES_ROWS, 2 * tq), F32),
            pltpu.VMEM((A_VDIM, 2 * tq), BF16),
            pltpu.VMEM((tk, 2 * tq), F32),
            pltpu.VMEM((tk, 2 * tq), F32),
            pltpu.VMEM((1, 2 * tq), F32),
            pltpu.VMEM((1, 2 * tq), F32),
        ],
        compiler_params=pltpu.CompilerParams(
            dimension_semantics=("arbitrary", "arbitrary"),
            vmem_limit_bytes=VMEM_LIMIT),
        name="diffattn",
    )(lam_vec, qt, kr, vt, p, a_norm_w.reshape(1, A_VDIM))


def _outproj_kernel(ym_ref, ya_ref, w_ref, x_ref, fw_ref, o_ref, *, final):
    yc = jnp.concatenate([ym_ref[...], ya_ref[...]], axis=1)
    xn = x_ref[...] + jnp.dot(yc, w_ref[...], preferred_element_type=F32)
    if final:
        ms = jnp.mean(xn * xn, axis=-1, keepdims=True)
        xn = xn * lax.rsqrt(ms + EPS) * fw_ref[...]
    o_ref[...] = xn


def _outproj(ym, ya, w_out, x2, final_w, *, final, tm=1024):
    rows = x2.shape[0]
    return pl.pallas_call(
        functools.partial(_outproj_kernel, final=final),
        grid=(rows // tm,),
        in_specs=[
            pl.BlockSpec((tm, M_WIDTH), lambda i: (i, 0)),
            pl.BlockSpec((tm, A_WIDTH), lambda i: (i, 0)),
            pl.BlockSpec((D_MODEL, D_MODEL), lambda i: (0, 0)),
            pl.BlockSpec((tm, D_MODEL), lambda i: (i, 0)),
            pl.BlockSpec((1, D_MODEL), lambda i: (0, 0)),
        ],
        out_specs=pl.BlockSpec((tm, D_MODEL), lambda i: (i, 0)),
        out_shape=jax.ShapeDtypeStruct((rows, D_MODEL), F32),
        compiler_params=pltpu.CompilerParams(
            dimension_semantics=("arbitrary",), vmem_limit_bytes=VMEM_LIMIT),
        name="outproj",
    )(ym, ya, w_out, x2, final_w.reshape(1, D_MODEL))


def _gate_lanes(i_part, f_part):
    rows = i_part.shape[0]
    z = lambda n: jnp.zeros((rows, n), i_part.dtype)
    return jnp.concatenate([i_part, z(STAT_ROWS - M_HEADS), f_part,
                            z(GATE_LANES - STAT_ROWS - M_HEADS)], axis=1)


def _rope_tables(seq):
    dh = A_QK_DIM
    inv = 1.0 / (ROPE_THETA ** (jnp.arange(0, dh, 2, dtype=F32) / dh))
    ang = jnp.arange(seq, dtype=F32)[:, None] * inv[None, :]
    cos = jnp.concatenate([jnp.cos(ang)] * 4, axis=-1)
    sin = jnp.sin(ang)
    sin = jnp.concatenate([-sin, sin, -sin, sin], axis=-1)
    return cos, sin


def kernel(x, norm_w, w_in, conv_w, conv_b, i_bias, f_bias, m_norm_w, m_skip,
           lam_q1, lam_k1, lam_q2, lam_k2, a_norm_w, w_out, final_norm_w):
    batch, seq, _ = x.shape
    depth = w_in.shape[0]
    x2 = x.reshape(batch * seq, D_MODEL)
    cos_t, sin_t = _rope_tables(seq)
    g0 = 3 * M_WIDTH
    g1 = g0 + 2 * M_HEADS
    for l in range(depth):
        lambda_init = 0.8 - 0.6 * math.exp(-0.3 * l)
        w_main = jnp.concatenate([w_in[l, :, :g0], w_in[l, :, g1:]], axis=1).astype(BF16)
        w_gate = _gate_lanes(w_in[l, :, g0:g0 + M_HEADS], w_in[l, :, g0 + M_HEADS:g1]).astype(BF16)
        gate_bias = _gate_lanes(i_bias[l][None, :], f_bias[l][None, :])
        lam_vec = jnp.stack([lam_q1[l], lam_k1[l], lam_q2[l], lam_k2[l]]).astype(F32)

        p, gates = _inproj(x2, norm_w[l], w_main, w_gate)
        ym = _mlstm(p, gates, conv_w[l], conv_b[l], gate_bias, m_norm_w[l], m_skip[l],
                    batch=batch, seq=seq)
        qt, kr, vt = _rope(p, cos_t, sin_t, batch=batch, seq=seq)
        ya = _attn(lam_vec, qt, kr, vt, p, a_norm_w[l], batch=batch, seq=seq,
                   lambda_init=lambda_init)
        x2 = _outproj(ym, ya, w_out[l].astype(BF16), x2, final_norm_w,
                      final=(l == depth - 1))
    return x2.reshape(batch, seq, D_MODEL)
```

```python
import functools
import math

import jax
import jax.numpy as jnp
from jax import lax
from jax.experimental import pallas as pl
from jax.experimental.pallas import tpu as pltpu

F32 = jnp.float32
BF16 = jnp.bfloat16

D_MODEL = 1024
M_WIDTH = 512
M_HEADS = 4
M_HEAD_DIM = 128
CONV_WIDTH = 4
A_WIDTH = 512
A_HEADS = 4
A_VDIM = 128
A_QK_DIM = 64
ATTN_CHUNK = 64
ROPE_THETA = 10000.0
EPS = 1e-6
LOG2E = 1.4426950408889634

P_WIDTH = 4096
GATE_LANES = 128
STAT_ROWS = 8

VMEM_LIMIT = 56 * 1024 * 1024
NEG_BIG = -1e30


def _silu(y):
    hy = 0.5 * y
    return hy + hy * jnp.tanh(hy)


def _inproj_kernel(x_ref, nw_ref, w_ref, wg_ref, p_ref, g_ref, h_scr):
    @pl.when(pl.program_id(1) == 0)
    def _():
        x = x_ref[...]
        ms = jnp.mean(x * x, axis=-1, keepdims=True)
        h = (x * lax.rsqrt(ms + EPS) * nw_ref[...]).astype(BF16)
        h_scr[...] = h
        g_ref[...] = jnp.dot(h, wg_ref[...], preferred_element_type=F32)

    p_ref[...] = jnp.dot(h_scr[...], w_ref[...], preferred_element_type=F32).astype(BF16)


def _inproj(x2, norm_w, w_main, w_gate, *, tm=1024, tn=2048):
    rows = x2.shape[0]
    return pl.pallas_call(
        _inproj_kernel,
        grid=(rows // tm, P_WIDTH // tn),
        in_specs=[
            pl.BlockSpec((tm, D_MODEL), lambda i, j: (i, 0)),
            pl.BlockSpec((1, D_MODEL), lambda i, j: (0, 0)),
            pl.BlockSpec((D_MODEL, tn), lambda i, j: (0, j)),
            pl.BlockSpec((D_MODEL, GATE_LANES), lambda i, j: (0, 0)),
        ],
        out_specs=[
            pl.BlockSpec((tm, tn), lambda i, j: (i, j)),
            pl.BlockSpec((tm, GATE_LANES), lambda i, j: (i, 0)),
        ],
        out_shape=[
            jax.ShapeDtypeStruct((rows, P_WIDTH), BF16),
            jax.ShapeDtypeStruct((rows, GATE_LANES), F32),
        ],
        scratch_shapes=[pltpu.VMEM((tm, D_MODEL), BF16)],
        compiler_params=pltpu.CompilerParams(
            dimension_semantics=("arbitrary", "arbitrary"),
            vmem_limit_bytes=VMEM_LIMIT),
        name="inproj",
    )(x2, norm_w.reshape(1, D_MODEL), w_main, w_gate)


def _dot_f32_exact_rhs01(x, tri):
    hi = x.astype(BF16)
    r1 = x - hi.astype(F32)
    mid = r1.astype(BF16)
    lo = (r1 - mid.astype(F32)).astype(BF16)
    d = lambda t: jnp.dot(t, tri, preferred_element_type=F32)
    return d(hi) + d(mid) + d(lo)


def _mlstm_kernel(qk_ref, v_ref, z_ref, g_ref, cw_ref, cb_ref, gb_ref, nw_ref, sk_ref,
                  o_ref, xbuf, qkc, hm_scr, c_scr, m_scr, *, L, batch):
    n = pl.program_id(0)
    D = M_HEAD_DIM
    R = STAT_ROWS
    bs = range(batch)

    @pl.when(n == 0)
    def _():
        xbuf[:, 0:8, :] = jnp.zeros((batch, 8, 2 * M_WIDTH), F32)
        c_scr[...] = jnp.zeros(c_scr.shape, F32)
        m_scr[...] = jnp.zeros(m_scr.shape, F32)

    row = lax.broadcasted_iota(jnp.int32, (L, L), 0)
    col = lax.broadcasted_iota(jnp.int32, (L, L), 1)
    causal = col <= row
    triu = jnp.where(row <= col, 1.0, 0.0).astype(BF16)
    lane = lax.broadcasted_iota(jnp.int32, (R, L), 1)

    def gate_stats(bi):
        gt = (g_ref[bi] + gb_ref[...]).T
        li = gt[0:R, :]
        fp = gt[R:2 * R, :]
        lf = jnp.minimum(fp, 0.0) - jnp.log1p(jnp.exp(-jnp.abs(fp)))
        b = _dot_f32_exact_rhs01(lf, triu)
        a = li - b
        pm = a
        shift = 1
        while shift < L:
            pm = jnp.maximum(pm, jnp.where(lane >= shift, pltpu.roll(pm, shift, 1), NEG_BIG))
            shift *= 2
        m_old = m_scr[bi]
        m_prev = jnp.concatenate([m_old] * (L // m_old.shape[1]), axis=1)
        m_i = jnp.maximum(b + m_prev, b + pm)
        w_inter = jnp.exp(b + m_prev - m_i)
        e_neg = jnp.exp(-m_i)
        b_last = b[:, L - 1:L]
        m_new = b_last + jnp.maximum(m_old[:, 0:1], pm[:, L - 1:L])
        decay = jnp.exp(b_last + m_old[:, 0:1] - m_new)
        wk = jnp.exp(b_last + a - m_new)
        m_scr[bi] = jnp.broadcast_to(m_new, m_old.shape)
        cols = jnp.concatenate([(m_i - b) * LOG2E, w_inter, e_neg, wk,
                                jnp.zeros((GATE_LANES - 4 * R, L), F32)], axis=0).T
        return a * LOG2E, cols, decay

    stats = []
    for bi in bs:
        stats.append(gate_stats(bi))
        xb = xbuf.at[bi]
        xb[8:8 + L, :] = qk_ref[bi].astype(F32)
        y = cb_ref[...] + cw_ref[3:4, :] * xb[8:8 + L, :]
        y = y + cw_ref[2:3, :] * xb[7:7 + L, :]
        y = y + cw_ref[1:2, :] * xb[6:6 + L, :]
        y = y + cw_ref[0:1, :] * xb[5:5 + L, :]
        xb[0:8, :] = xb[L:L + 8, :]
        qkc[bi] = _silu(y)

    ones = jnp.ones((L, D), BF16)
    for h in range(M_HEADS):
        sl = slice(h * D, (h + 1) * D)
        for bi in bs:
            a2, cols, decay = stats[bi]
            c_col = cols[:, h:h + 1]
            wi_col = cols[:, R + h:R + h + 1]
            en_col = cols[:, 2 * R + h:2 * R + h + 1]
            wk_col = cols[:, 3 * R + h:3 * R + h + 1]
            w_intra = jnp.where(causal, jnp.exp2(a2[h:h + 1, :] - c_col), 0.0)

            qb = qkc[bi, :, sl].astype(BF16)
            kh = qkc[bi, :, M_WIDTH + h * D:M_WIDTH + (h + 1) * D] * (D ** -0.5)
            v_ext = jnp.concatenate([v_ref[bi, :, sl], ones], axis=1)
            s = lax.dot_general(qb, kh.astype(BF16), (((1,), (1,)), ((), ())),
                                preferred_element_type=F32) * w_intra
            c_old = c_scr[bi, h]
            q_c = jnp.dot(qb, c_old.astype(BF16), preferred_element_type=F32)
            s_v = jnp.dot(s.astype(BF16), v_ext, preferred_element_type=F32)
            num = wi_col * q_c[:, :D] + s_v[:, :D]
            den = wi_col * q_c[:, D:D + 1] + s_v[:, D:D + 1]
            hm_scr[bi, :, sl] = num * (1.0 / jnp.maximum(jnp.abs(den), en_col))

            kw_t = (kh * wk_col).T.astype(BF16)
            c_scr[bi, h] = decay[h:h + 1, :] * c_old + jnp.dot(kw_t, v_ext,
                                                               preferred_element_type=F32)

    for h in range(M_HEADS):
        sl = slice(h * D, (h + 1) * D)
        for bi in bs:
            hm = hm_scr[bi, :, sl]
            ms = jnp.mean(hm * hm, axis=-1, keepdims=True)
            hn = hm * lax.rsqrt(ms + EPS) * nw_ref[:, sl]
            ym = (hn + sk_ref[:, sl] * qkc[bi, :, sl]) * _silu(z_ref[bi, :, sl].astype(F32))
            o_ref[bi, :, sl] = ym.astype(BF16)


def _mlstm(p, gates, conv_w, conv_b, gate_bias, m_norm_w, m_skip, *, batch, seq, L=256):
    nchunk = seq // L
    p3 = p.reshape(batch, seq, P_WIDTH)
    g3 = gates.reshape(batch, seq, GATE_LANES)
    const = lambda n: (0, 0)
    ym = pl.pallas_call(
        functools.partial(_mlstm_kernel, L=L, batch=batch),
        grid=(nchunk,),
        in_specs=[
            pl.BlockSpec((batch, L, 2 * M_WIDTH), lambda n: (0, n, 0)),
            pl.BlockSpec((batch, L, M_WIDTH), lambda n: (0, n, 2)),
            pl.BlockSpec((batch, L, M_WIDTH), lambda n: (0, n, 3)),
            pl.BlockSpec((batch, L, GATE_LANES), lambda n: (0, n, 0)),
            pl.BlockSpec((CONV_WIDTH, 2 * M_WIDTH), const),
            pl.BlockSpec((1, 2 * M_WIDTH), const),
            pl.BlockSpec((1, GATE_LANES), const),
            pl.BlockSpec((1, M_WIDTH), const),
            pl.BlockSpec((1, M_WIDTH), const),
        ],
        out_specs=pl.BlockSpec((batch, L, M_WIDTH), lambda n: (0, n, 0)),
        out_shape=jax.ShapeDtypeStruct((batch, seq, M_WIDTH), BF16),
        scratch_shapes=[
            pltpu.VMEM((batch, L + 8, 2 * M_WIDTH), F32),
            pltpu.VMEM((batch, L, 2 * M_WIDTH), F32),
            pltpu.VMEM((batch, L, M_WIDTH), F32),
            pltpu.VMEM((batch, M_HEADS, M_HEAD_DIM, 2 * M_HEAD_DIM), F32),
            pltpu.VMEM((batch, STAT_ROWS, 128), F32),
        ],
        compiler_params=pltpu.CompilerParams(
            dimension_semantics=("arbitrary",),
            vmem_limit_bytes=VMEM_LIMIT),
        name="mlstm",
    )(p3, p3, p3, g3, conv_w, conv_b.reshape(1, -1), gate_bias,
      m_norm_w.reshape(1, -1), m_skip.reshape(1, -1))
    return ym.reshape(batch * seq, M_WIDTH)


def _rope_kernel(q_ref, k_ref, v_ref, cos_ref, sin_ref, qt_ref, ko_ref, vt_ref, *, q_scale):
    cos = jnp.concatenate([cos_ref[...]] * A_HEADS, axis=1)
    sin = jnp.concatenate([sin_ref[...]] * A_HEADS, axis=1)
    lane = lax.broadcasted_iota(jnp.int32, cos.shape, 1)
    first_half = (lane & (A_QK_DIM - 1)) < (A_QK_DIM // 2)

    def rot(x):
        fwd = pltpu.roll(x, A_QK_DIM // 2, 1)
        bwd = pltpu.roll(x, A_WIDTH - A_QK_DIM // 2, 1)
        return x * cos + jnp.where(first_half, bwd, fwd) * sin

    qt_ref[...] = (rot(q_ref[...].astype(F32)) * q_scale).T.astype(BF16)
    ko_ref[...] = rot(k_ref[...].astype(F32)).astype(BF16)
    vt_ref[...] = v_ref[...].astype(F32).T.astype(BF16)


def _rope(p, cos_t, sin_t, *, batch, seq, tm=1024):
    rows = p.shape[0]
    nsb = seq // tm
    q_scale = (A_QK_DIM ** -0.5) * LOG2E
    nat = jax.ShapeDtypeStruct((rows, A_WIDTH), BF16)
    tr = jax.ShapeDtypeStruct((batch * A_WIDTH, seq), BF16)
    tr_spec = pl.BlockSpec((A_WIDTH, tm), lambda i: (i // nsb, i % nsb))
    return pl.pallas_call(
        functools.partial(_rope_kernel, q_scale=q_scale),
        grid=(rows // tm,),
        in_specs=[
            pl.BlockSpec((tm, A_WIDTH), lambda i: (i, 4)),
            pl.BlockSpec((tm, A_WIDTH), lambda i: (i, 5)),
            pl.BlockSpec((tm, A_WIDTH), lambda i: (i, 6)),
            pl.BlockSpec((tm, 2 * A_QK_DIM), lambda i: (i % nsb, 0)),
            pl.BlockSpec((tm, 2 * A_QK_DIM), lambda i: (i % nsb, 0)),
        ],
        out_specs=[tr_spec, pl.BlockSpec((tm, A_WIDTH), lambda i: (i, 0)), tr_spec],
        out_shape=[tr, nat, tr],
        compiler_params=pltpu.CompilerParams(
            dimension_semantics=("arbitrary",), vmem_limit_bytes=VMEM_LIMIT),
        name="rope",
    )(p, p, p, cos_t, sin_t)


ONES_ROWS = 16
QGROUP = 256
SCORE_PAD = 128


def _attn_kernel(lam_ref, qt_ref, k_ref, vt_ref, z_ref, nw_ref, o_ref,
                 m_scr, acc_scr, q2_scr, sa, sb, cma, cmb, *, tq, tk, nq, lambda_init):
    dv = A_VDIM
    ng = 2 * tq // QGROUP
    ones = jnp.ones((ONES_ROWS, tk), BF16)

    def load_queries(i):
        qt = qt_ref[:, pl.ds(pl.multiple_of(i * tq, tq), tq)]
        rowi = lax.broadcasted_iota(jnp.int32, qt.shape, 0)
        zero = jnp.zeros_like(qt)
        q2_scr[:, 0:tq] = jnp.where(rowi < A_QK_DIM, qt, zero)
        q2_scr[:, tq:] = jnp.where(rowi >= A_QK_DIM, qt, zero)

    def reset_stats():
        m_scr[...] = jnp.full(m_scr.shape, NEG_BIG, F32)
        acc_scr[...] = jnp.zeros(acc_scr.shape, F32)

    def all_masked(diag_off, g):
        return diag_off is not None and diag_off >= (g * QGROUP) % tq + QGROUP

    def scores_group(start, s_ref, cm_ref, diag_off, g):
        gs = slice(g * QGROUP, (g + 1) * QGROUP)
        qlo = (g * QGROUP) % tq
        if all_masked(diag_off, g):
            return
        k = k_ref[pl.ds(start, tk), :]
        st = jnp.dot(k, q2_scr[:, gs], preferred_element_type=F32)
        if diag_off is not None and diag_off + tk > qlo + ATTN_CHUNK:
            kk = lax.broadcasted_iota(jnp.int32, st.shape, 0) + diag_off
            qq = lax.broadcasted_iota(jnp.int32, st.shape, 1) + qlo
            st = jnp.where((kk // ATTN_CHUNK) <= (qq // ATTN_CHUNK), st, NEG_BIG)
        s_ref[:, gs] = st
        cm_ref[:, gs] = jnp.max(st, axis=0, keepdims=True)

    def consume_group(start, s_ref, cm_ref, diag_off, g):
        gs = slice(g * QGROUP, (g + 1) * QGROUP)
        if all_masked(diag_off, g):
            return
        vt_ext = jnp.concatenate([vt_ref[:, pl.ds(start, tk)], ones], axis=0)
        m_old = m_scr[:, gs]
        m_new = jnp.maximum(m_old, cm_ref[:, gs])
        alpha = jnp.exp2(m_old - m_new)
        pt = jnp.exp2((s_ref[:, gs] - m_new).astype(BF16))
        acc_scr[:, gs] = alpha * acc_scr[:, gs] + jnp.dot(vt_ext, pt,
                                                          preferred_element_type=F32)
        m_scr[:, gs] = m_new

    def stage(score_args, consume_args):
        for g in range(ng):
            if score_args is not None:
                scores_group(*score_args, g)
            if consume_args is not None:
                consume_group(*consume_args, g)

    lv = lam_ref[...]
    lam = (jnp.exp(jnp.sum(lv[0:1, :] * lv[1:2, :], axis=1, keepdims=True))
           - jnp.exp(jnp.sum(lv[2:3, :] * lv[3:4, :], axis=1, keepdims=True))
           + lambda_init)

    def finalize(i):
        rows = pl.ds(pl.multiple_of(i * tq, tq), tq)
        o0 = acc_scr[0:dv, 0:tq] / acc_scr[dv:dv + 1, 0:tq]
        o1 = acc_scr[0:dv, tq:] / acc_scr[dv:dv + 1, tq:]
        out_t = o0 - lam * o1
        ms = jnp.mean(out_t * out_t, axis=0, keepdims=True)
        hn = (out_t * lax.rsqrt(ms + EPS)).T
        hn = hn * nw_ref[...] * (1.0 - lambda_init)
        o_ref[rows, :] = (hn * _silu(z_ref[rows, :].astype(F32))).astype(BF16)

    def off(x):
        return pl.multiple_of(x, tk)

    def diag_tail(i):
        d0 = i * tq
        stage((off(d0 + tk), sb, cmb, tk), (off(d0), sa, cma, 0))
        load_queries(jnp.minimum(i + 1, nq - 1))
        stage((0, sa, cma, None), (off(d0 + tk), sb, cmb, tk))
        finalize(i)
        reset_stats()

    reset_stats()
    load_queries(0)
    stage((0, sa, cma, 0), None)
    diag_tail(0)

    def q_block(i, carry):
        def body(t, c):
            u = 2 * t * tk
            stage((off(u + tk), sb, cmb, None), (off(u), sa, cma, None))
            stage((off(u + 2 * tk), sa, cma, None), (off(u + tk), sb, cmb, None))
            return c

        lax.fori_loop(0, i - 1, body, 0)
        u = (2 * i - 2) * tk
        stage((off(u + tk), sb, cmb, None), (off(u), sa, cma, None))
        stage((off(i * tq), sa, cma, 0), (off(u + tk), sb, cmb, None))
        diag_tail(i)
        return carry

    lax.fori_loop(1, nq, q_block, 0)


def _attn(lam_vec, qt, kr, vt, p, a_norm_w, *, batch, seq, lambda_init, tq=1024):
    nq = seq // tq
    tk = tq // 2
    z_col0 = 3584 // A_VDIM
    return pl.pallas_call(
        functools.partial(_attn_kernel, tq=tq, tk=tk, nq=nq, lambda_init=lambda_init),
        grid=(batch, A_HEADS),
        in_specs=[
            pl.BlockSpec((4, A_QK_DIM), lambda b, h: (0, 0)),
            pl.BlockSpec((A_VDIM, seq), lambda b, h: (b * A_HEADS + h, 0)),
            pl.BlockSpec((seq, A_VDIM), lambda b, h: (b, h)),
            pl.BlockSpec((A_VDIM, seq), lambda b, h: (b * A_HEADS + h, 0)),
            pl.BlockSpec((seq, A_VDIM), lambda b, h: (b, z_col0 + h)),
            pl.BlockSpec((1, A_VDIM), lambda b, h: (0, 0)),
        ],
        out_specs=pl.BlockSpec((seq, A_VDIM), lambda b, h: (b, h)),
        out_shape=jax.ShapeDtypeStruct((batch * seq, A_WIDTH), BF16),
        scratch_shapes=[
            pltpu.VMEM((1, 2 * tq), F32),
            pltpu.VMEM((A_VDIM + ONES_ROWS, 2 * tq), F32),
            pltpu.VMEM((A_VDIM, 2 * tq), BF16),
            pltpu.VMEM((tk, 2 * tq + SCORE_PAD), F32),
            pltpu.VMEM((tk, 2 * tq + SCORE_PAD), F32),
            pltpu.VMEM((1, 2 * tq), F32),
            pltpu.VMEM((1, 2 * tq), F32),
        ],
        compiler_params=pltpu.CompilerParams(
            dimension_semantics=("arbitrary", "arbitrary"),
            vmem_limit_bytes=VMEM_LIMIT),
        name="diffattn",
    )(lam_vec, qt, kr, vt, p, a_norm_w.reshape(1, A_VDIM))


def _outproj_kernel(ym_ref, ya_ref, w_ref, x_ref, fw_ref, o_ref, *, final):
    yc = jnp.concatenate([ym_ref[...], ya_ref[...]], axis=1)
    xn = x_ref[...] + jnp.dot(yc, w_ref[...], preferred_element_type=F32)
    if final:
        ms = jnp.mean(xn * xn, axis=-1, keepdims=True)
        xn = xn * lax.rsqrt(ms + EPS) * fw_ref[...]
    o_ref[...] = xn


def _outproj(ym, ya, w_out, x2, final_w, *, final, tm=1024):
    rows = x2.shape[0]
    return pl.pallas_call(
        functools.partial(_outproj_kernel, final=final),
        grid=(rows // tm,),
        in_specs=[
            pl.BlockSpec((tm, M_WIDTH), lambda i: (i, 0)),
            pl.BlockSpec((tm, A_WIDTH), lambda i: (i, 0)),
            pl.BlockSpec((D_MODEL, D_MODEL), lambda i: (0, 0)),
            pl.BlockSpec((tm, D_MODEL), lambda i: (i, 0)),
            pl.BlockSpec((1, D_MODEL), lambda i: (0, 0)),
        ],
        out_specs=pl.BlockSpec((tm, D_MODEL), lambda i: (i, 0)),
        out_shape=jax.ShapeDtypeStruct((rows, D_MODEL), F32),
        compiler_params=pltpu.CompilerParams(
            dimension_semantics=("arbitrary",), vmem_limit_bytes=VMEM_LIMIT),
        name="outproj",
    )(ym, ya, w_out, x2, final_w.reshape(1, D_MODEL))


def _gate_lanes(i_part, f_part):
    rows = i_part.shape[0]
    z = lambda n: jnp.zeros((rows, n), i_part.dtype)
    return jnp.concatenate([i_part, z(STAT_ROWS - M_HEADS), f_part,
                            z(GATE_LANES - STAT_ROWS - M_HEADS)], axis=1)


def _rope_tables(seq):
    dh = A_QK_DIM
    inv = 1.0 / (ROPE_THETA ** (jnp.arange(0, dh, 2, dtype=F32) / dh))
    ang = jnp.arange(seq, dtype=F32)[:, None] * inv[None, :]
    cos = jnp.concatenate([jnp.cos(ang)] * 4, axis=-1)
    sin = jnp.sin(ang)
    sin = jnp.concatenate([-sin, sin, -sin, sin], axis=-1)
    return cos, sin


def kernel(x, norm_w, w_in, conv_w, conv_b, i_bias, f_bias, m_norm_w, m_skip,
           lam_q1, lam_k1, lam_q2, lam_k2, a_norm_w, w_out, final_norm_w):
    batch, seq, _ = x.shape
    depth = w_in.shape[0]
    x2 = x.reshape(batch * seq, D_MODEL)
    cos_t, sin_t = _rope_tables(seq)
    g0 = 3 * M_WIDTH
    g1 = g0 + 2 * M_HEADS
    for l in range(depth):
        lambda_init = 0.8 - 0.6 * math.exp(-0.3 * l)
        w_main = jnp.concatenate([w_in[l, :, :g0], w_in[l, :, g1:]], axis=1).astype(BF16)
        w_gate = _gate_lanes(w_in[l, :, g0:g0 + M_HEADS], w_in[l, :, g0 + M_HEADS:g1]).astype(BF16)
        gate_bias = _gate_lanes(i_bias[l][None, :], f_bias[l][None, :])
        lam_vec = jnp.stack([lam_q1[l], lam_k1[l], lam_q2[l], lam_k2[l]]).astype(F32)

        p, gates = _inproj(x2, norm_w[l], w_main, w_gate)
        ym = _mlstm(p, gates, conv_w[l], conv_b[l], gate_bias, m_norm_w[l], m_skip[l],
                    batch=batch, seq=seq)
        qt, kr, vt = _rope(p, cos_t, sin_t, batch=batch, seq=seq)
        ya = _attn(lam_vec, qt, kr, vt, p, a_norm_w[l], batch=batch, seq=seq,
                   lambda_init=lambda_init)
        x2 = _outproj(ym, ya, w_out[l].astype(BF16), x2, final_norm_w,
                      final=(l == depth - 1))
    return x2.reshape(batch, seq, D_MODEL)
```

```python
import functools
import math

import jax
import jax.numpy as jnp
from jax import lax
from jax.experimental import pallas as pl
from jax.experimental.pallas import tpu as pltpu

F32 = jnp.float32
BF16 = jnp.bfloat16

D_MODEL = 1024
M_WIDTH = 512
M_HEADS = 4
M_HEAD_DIM = 128
CONV_WIDTH = 4
A_WIDTH = 512
A_HEADS = 4
A_VDIM = 128
A_QK_DIM = 64
ATTN_CHUNK = 64
ROPE_THETA = 10000.0
ROPE_BLOCK = 64
EPS = 1e-6
LOG2E = 1.4426950408889634

P_WIDTH = 4096
GATE_LANES = 128
STAT_ROWS = 8

VMEM_LIMIT = 56 * 1024 * 1024
NEG_BIG = -1e30


def _silu(y):
    hy = 0.5 * y
    return hy + hy * jnp.tanh(hy)


def _inproj_kernel(x_ref, nw_ref, w_ref, wg_ref, p_ref, g_ref, h_scr):
    @pl.when(pl.program_id(1) == 0)
    def _():
        x = x_ref[...]
        ms = jnp.mean(x * x, axis=-1, keepdims=True)
        h = (x * lax.rsqrt(ms + EPS) * nw_ref[...]).astype(BF16)
        h_scr[...] = h
        g_ref[...] = jnp.dot(h, wg_ref[...], preferred_element_type=F32)

    p_ref[...] = jnp.dot(h_scr[...], w_ref[...], preferred_element_type=F32).astype(BF16)


def _inproj(x2, norm_w, w_main, w_gate, *, tm=1024, tn=2048):
    rows = x2.shape[0]
    return pl.pallas_call(
        _inproj_kernel,
        grid=(rows // tm, P_WIDTH // tn),
        in_specs=[
            pl.BlockSpec((tm, D_MODEL), lambda i, j: (i, 0)),
            pl.BlockSpec((1, D_MODEL), lambda i, j: (0, 0)),
            pl.BlockSpec((D_MODEL, tn), lambda i, j: (0, j)),
            pl.BlockSpec((D_MODEL, GATE_LANES), lambda i, j: (0, 0)),
        ],
        out_specs=[
            pl.BlockSpec((tm, tn), lambda i, j: (i, j)),
            pl.BlockSpec((tm, GATE_LANES), lambda i, j: (i, 0)),
        ],
        out_shape=[
            jax.ShapeDtypeStruct((rows, P_WIDTH), BF16),
            jax.ShapeDtypeStruct((rows, GATE_LANES), F32),
        ],
        scratch_shapes=[pltpu.VMEM((tm, D_MODEL), BF16)],
        compiler_params=pltpu.CompilerParams(
            dimension_semantics=("arbitrary", "arbitrary"),
            vmem_limit_bytes=VMEM_LIMIT),
        name="inproj",
    )(x2, norm_w.reshape(1, D_MODEL), w_main, w_gate)


def _dot_f32_exact_rhs01(x, tri):
    hi = x.astype(BF16)
    r1 = x - hi.astype(F32)
    mid = r1.astype(BF16)
    lo = (r1 - mid.astype(F32)).astype(BF16)
    d = lambda t: jnp.dot(t, tri, preferred_element_type=F32)
    return d(hi) + d(mid) + d(lo)


def _mlstm_kernel(qk_ref, v_ref, z_ref, g_ref, cw_ref, cb_ref, gb_ref, nw_ref, sk_ref,
                  o_ref, xbuf, qkc, hm_scr, c_scr, m_scr, *, L, batch):
    n = pl.program_id(0)
    D = M_HEAD_DIM
    R = STAT_ROWS
    bs = range(batch)

    @pl.when(n == 0)
    def _():
        xbuf[:, 0:8, :] = jnp.zeros((batch, 8, 2 * M_WIDTH), F32)
        c_scr[...] = jnp.zeros(c_scr.shape, F32)
        m_scr[...] = jnp.zeros(m_scr.shape, F32)

    row = lax.broadcasted_iota(jnp.int32, (L, L), 0)
    col = lax.broadcasted_iota(jnp.int32, (L, L), 1)
    causal = col <= row
    triu = jnp.where(row <= col, 1.0, 0.0).astype(BF16)
    lane = lax.broadcasted_iota(jnp.int32, (R, L), 1)

    def gate_stats(bi):
        gt = (g_ref[bi] + gb_ref[...]).T
        li = gt[0:R, :]
        fp = gt[R:2 * R, :]
        lf = jnp.minimum(fp, 0.0) - jnp.log1p(jnp.exp(-jnp.abs(fp)))
        b = _dot_f32_exact_rhs01(lf, triu)
        a = li - b
        pm = a
        shift = 1
        while shift < L:
            pm = jnp.maximum(pm, jnp.where(lane >= shift, pltpu.roll(pm, shift, 1), NEG_BIG))
            shift *= 2
        m_old = m_scr[bi]
        m_prev = jnp.concatenate([m_old] * (L // m_old.shape[1]), axis=1)
        m_i = jnp.maximum(b + m_prev, b + pm)
        w_inter = jnp.exp(b + m_prev - m_i)
        e_neg = jnp.exp(-m_i)
        b_last = b[:, L - 1:L]
        m_new = b_last + jnp.maximum(m_old[:, 0:1], pm[:, L - 1:L])
        decay = jnp.exp(b_last + m_old[:, 0:1] - m_new)
        wk = jnp.exp(b_last + a - m_new)
        m_scr[bi] = jnp.broadcast_to(m_new, m_old.shape)
        cols = jnp.concatenate([(m_i - b) * LOG2E, w_inter, e_neg, wk,
                                jnp.zeros((GATE_LANES - 4 * R, L), F32)], axis=0).T
        return a * LOG2E, cols, decay

    stats = []
    for bi in bs:
        stats.append(gate_stats(bi))
        xb = xbuf.at[bi]
        xb[8:8 + L, :] = qk_ref[bi].astype(F32)
        y = cb_ref[...] + cw_ref[3:4, :] * xb[8:8 + L, :]
        y = y + cw_ref[2:3, :] * xb[7:7 + L, :]
        y = y + cw_ref[1:2, :] * xb[6:6 + L, :]
        y = y + cw_ref[0:1, :] * xb[5:5 + L, :]
        xb[0:8, :] = xb[L:L + 8, :]
        qkc[bi] = _silu(y)

    ones = jnp.ones((L, D), BF16)
    for h in range(M_HEADS):
        sl = slice(h * D, (h + 1) * D)
        for bi in bs:
            a2, cols, decay = stats[bi]
            c_col = cols[:, h:h + 1]
            wi_col = cols[:, R + h:R + h + 1]
            en_col = cols[:, 2 * R + h:2 * R + h + 1]
            wk_col = cols[:, 3 * R + h:3 * R + h + 1]
            w_intra = jnp.where(causal, jnp.exp2(a2[h:h + 1, :] - c_col), 0.0)

            qb = qkc[bi, :, sl].astype(BF16)
            kh = qkc[bi, :, M_WIDTH + h * D:M_WIDTH + (h + 1) * D] * (D ** -0.5)
            v_ext = jnp.concatenate([v_ref[bi, :, sl], ones], axis=1)
            s = lax.dot_general(qb, kh.astype(BF16), (((1,), (1,)), ((), ())),
                                preferred_element_type=F32) * w_intra
            c_old = c_scr[bi, h]
            q_c = jnp.dot(qb, c_old.astype(BF16), preferred_element_type=F32)
            s_v = jnp.dot(s.astype(BF16), v_ext, preferred_element_type=F32)
            num = wi_col * q_c[:, :D] + s_v[:, :D]
            den = wi_col * q_c[:, D:D + 1] + s_v[:, D:D + 1]
            hm_scr[bi, :, sl] = num * (1.0 / jnp.maximum(jnp.abs(den), en_col))

            kw_t = (kh * wk_col).T.astype(BF16)
            c_scr[bi, h] = decay[h:h + 1, :] * c_old + jnp.dot(kw_t, v_ext,
                                                               preferred_element_type=F32)

    for h in range(M_HEADS):
        sl = slice(h * D, (h + 1) * D)
        for bi in bs:
            hm = hm_scr[bi, :, sl]
            ms = jnp.mean(hm * hm, axis=-1, keepdims=True)
            hn = hm * lax.rsqrt(ms + EPS) * nw_ref[:, sl]
            ym = (hn + sk_ref[:, sl] * qkc[bi, :, sl]) * _silu(z_ref[bi, :, sl].astype(F32))
            o_ref[bi, :, sl] = ym.astype(BF16)


def _mlstm(p, gates, conv_w, conv_b, gate_bias, m_norm_w, m_skip, *, batch, seq, L=256):
    nchunk = seq // L
    p3 = p.reshape(batch, seq, P_WIDTH)
    g3 = gates.reshape(batch, seq, GATE_LANES)
    const = lambda n: (0, 0)
    ym = pl.pallas_call(
        functools.partial(_mlstm_kernel, L=L, batch=batch),
        grid=(nchunk,),
        in_specs=[
            pl.BlockSpec((batch, L, 2 * M_WIDTH), lambda n: (0, n, 0)),
            pl.BlockSpec((batch, L, M_WIDTH), lambda n: (0, n, 2)),
            pl.BlockSpec((batch, L, M_WIDTH), lambda n: (0, n, 3)),
            pl.BlockSpec((batch, L, GATE_LANES), lambda n: (0, n, 0)),
            pl.BlockSpec((CONV_WIDTH, 2 * M_WIDTH), const),
            pl.BlockSpec((1, 2 * M_WIDTH), const),
            pl.BlockSpec((1, GATE_LANES), const),
            pl.BlockSpec((1, M_WIDTH), const),
            pl.BlockSpec((1, M_WIDTH), const),
        ],
        out_specs=pl.BlockSpec((batch, L, M_WIDTH), lambda n: (0, n, 0)),
        out_shape=jax.ShapeDtypeStruct((batch, seq, M_WIDTH), BF16),
        scratch_shapes=[
            pltpu.VMEM((batch, L + 8, 2 * M_WIDTH), F32),
            pltpu.VMEM((batch, L, 2 * M_WIDTH), F32),
            pltpu.VMEM((batch, L, M_WIDTH), F32),
            pltpu.VMEM((batch, M_HEADS, M_HEAD_DIM, 2 * M_HEAD_DIM), F32),
            pltpu.VMEM((batch, STAT_ROWS, 128), F32),
        ],
        compiler_params=pltpu.CompilerParams(
            dimension_semantics=("arbitrary",),
            vmem_limit_bytes=VMEM_LIMIT),
        name="mlstm",
    )(p3, p3, p3, g3, conv_w, conv_b.reshape(1, -1), gate_bias,
      m_norm_w.reshape(1, -1), m_skip.reshape(1, -1))
    return ym.reshape(batch * seq, M_WIDTH)


def _rope_kernel(q_ref, k_ref, v_ref, cos_ref, sin_ref, qt_ref, ko_ref, vt_ref, *, q_scale):
    cos = jnp.concatenate([cos_ref[...]] * A_HEADS, axis=1)
    sin = jnp.concatenate([sin_ref[...]] * A_HEADS, axis=1)
    lane = lax.broadcasted_iota(jnp.int32, cos.shape, 1)
    first_half = (lane & (A_QK_DIM - 1)) < (A_QK_DIM // 2)

    def rot(x):
        fwd = pltpu.roll(x, A_QK_DIM // 2, 1)
        bwd = pltpu.roll(x, A_WIDTH - A_QK_DIM // 2, 1)
        return x * cos + jnp.where(first_half, bwd, fwd) * sin

    qt_ref[...] = (rot(q_ref[...].astype(F32)) * q_scale).T.astype(BF16)
    ko_ref[...] = rot(k_ref[...].astype(F32)).astype(BF16)
    vt_ref[...] = v_ref[...].astype(F32).T.astype(BF16)


def _rope(p, cos_t, sin_t, *, batch, seq, tm=1024):
    rows = p.shape[0]
    nsb = seq // tm
    q_scale = (A_QK_DIM ** -0.5) * LOG2E
    nat = jax.ShapeDtypeStruct((rows, A_WIDTH), BF16)
    tr = jax.ShapeDtypeStruct((batch * A_WIDTH, seq), BF16)
    tr_spec = pl.BlockSpec((A_WIDTH, tm), lambda i: (i // nsb, i % nsb))
    return pl.pallas_call(
        functools.partial(_rope_kernel, q_scale=q_scale),
        grid=(rows // tm,),
        in_specs=[
            pl.BlockSpec((tm, A_WIDTH), lambda i: (i, 4)),
            pl.BlockSpec((tm, A_WIDTH), lambda i: (i, 5)),
            pl.BlockSpec((tm, A_WIDTH), lambda i: (i, 6)),
            pl.BlockSpec((tm, 2 * A_QK_DIM), lambda i: (i % nsb, 0)),
            pl.BlockSpec((tm, 2 * A_QK_DIM), lambda i: (i % nsb, 0)),
        ],
        out_specs=[tr_spec, pl.BlockSpec((tm, A_WIDTH), lambda i: (i, 0)), tr_spec],
        out_shape=[tr, nat, tr],
        compiler_params=pltpu.CompilerParams(
            dimension_semantics=("arbitrary",), vmem_limit_bytes=VMEM_LIMIT),
        name="rope",
    )(p, p, p, cos_t, sin_t)


ONES_ROWS = 16
QGROUP = 256
SCORE_PAD = 128


def _attn_kernel(lam_ref, qt_ref, k_ref, vt_ref, z_ref, nw_ref, o_ref,
                 m_scr, acc_scr, q2_scr, sa, sb, cma, cmb, *, tq, tk, nq, lambda_init):
    dv = A_VDIM
    ng = 2 * tq // QGROUP
    ones = jnp.ones((ONES_ROWS, tk), BF16)

    def load_queries(i):
        qt = qt_ref[:, pl.ds(pl.multiple_of(i * tq, tq), tq)]
        rowi = lax.broadcasted_iota(jnp.int32, qt.shape, 0)
        zero = jnp.zeros_like(qt)
        q2_scr[:, 0:tq] = jnp.where(rowi < A_QK_DIM, qt, zero)
        q2_scr[:, tq:] = jnp.where(rowi >= A_QK_DIM, qt, zero)

    def reset_stats():
        m_scr[...] = jnp.full(m_scr.shape, NEG_BIG, F32)
        acc_scr[...] = jnp.zeros(acc_scr.shape, F32)

    def all_masked(diag_off, g):
        return diag_off is not None and diag_off >= (g * QGROUP) % tq + QGROUP

    def scores_group(start, s_ref, cm_ref, diag_off, g):
        gs = slice(g * QGROUP, (g + 1) * QGROUP)
        qlo = (g * QGROUP) % tq
        if all_masked(diag_off, g):
            return
        k = k_ref[pl.ds(start, tk), :]
        st = jnp.dot(k, q2_scr[:, gs], preferred_element_type=F32)
        if diag_off is not None and diag_off + tk > qlo + ATTN_CHUNK:
            kk = lax.broadcasted_iota(jnp.int32, st.shape, 0) + diag_off
            qq = lax.broadcasted_iota(jnp.int32, st.shape, 1) + qlo
            st = jnp.where((kk // ATTN_CHUNK) <= (qq // ATTN_CHUNK), st, NEG_BIG)
        s_ref[:, gs] = st
        cm_ref[:, gs] = jnp.max(st, axis=0, keepdims=True)

    def consume_group(start, s_ref, cm_ref, diag_off, g):
        gs = slice(g * QGROUP, (g + 1) * QGROUP)
        if all_masked(diag_off, g):
            return
        vt_ext = jnp.concatenate([vt_ref[:, pl.ds(start, tk)], ones], axis=0)
        m_old = m_scr[:, gs]
        m_new = jnp.maximum(m_old, cm_ref[:, gs])
        alpha = jnp.exp2(m_old - m_new)
        pt = jnp.exp2((s_ref[:, gs] - m_new).astype(BF16))
        acc_scr[:, gs] = alpha * acc_scr[:, gs] + jnp.dot(vt_ext, pt,
                                                          preferred_element_type=F32)
        m_scr[:, gs] = m_new

    def stage(score_args, consume_args):
        for g in range(ng):
            if score_args is not None:
                scores_group(*score_args, g)
            if consume_args is not None:
                consume_group(*consume_args, g)

    lv = lam_ref[...]
    lam = (jnp.exp(jnp.sum(lv[0:1, :] * lv[1:2, :], axis=1, keepdims=True))
           - jnp.exp(jnp.sum(lv[2:3, :] * lv[3:4, :], axis=1, keepdims=True))
           + lambda_init)

    def finalize(i):
        rows = pl.ds(pl.multiple_of(i * tq, tq), tq)
        o0 = acc_scr[0:dv, 0:tq] / acc_scr[dv:dv + 1, 0:tq]
        o1 = acc_scr[0:dv, tq:] / acc_scr[dv:dv + 1, tq:]
        out_t = o0 - lam * o1
        ms = jnp.mean(out_t * out_t, axis=0, keepdims=True)
        hn = (out_t * lax.rsqrt(ms + EPS)).T
        hn = hn * nw_ref[...] * (1.0 - lambda_init)
        o_ref[rows, :] = (hn * _silu(z_ref[rows, :].astype(F32))).astype(BF16)

    def off(x):
        return pl.multiple_of(x, tk)

    def diag_tail(i):
        d0 = i * tq
        stage((off(d0 + tk), sb, cmb, tk), (off(d0), sa, cma, 0))
        load_queries(jnp.minimum(i + 1, nq - 1))
        stage((0, sa, cma, None), (off(d0 + tk), sb, cmb, tk))
        finalize(i)
        reset_stats()

    reset_stats()
    load_queries(0)
    stage((0, sa, cma, 0), None)
    diag_tail(0)

    def q_block(i, carry):
        def two_steps(t):
            u = 2 * t * tk
            stage((off(u + tk), sb, cmb, None), (off(u), sa, cma, None))
            stage((off(u + 2 * tk), sa, cma, None), (off(u + tk), sb, cmb, None))

        def body(t, c):
            two_steps(2 * t)
            two_steps(2 * t + 1)
            return c

        n_two = i - 1
        lax.fori_loop(0, lax.shift_right_logical(n_two, 1), body, 0)

        @pl.when((n_two & 1) == 1)
        def _():
            two_steps(n_two - 1)

        u = (2 * i - 2) * tk
        stage((off(u + tk), sb, cmb, None), (off(u), sa, cma, None))
        stage((off(i * tq), sa, cma, 0), (off(u + tk), sb, cmb, None))
        diag_tail(i)
        return carry

    lax.fori_loop(1, nq, q_block, 0)


def _attn(lam_vec, qt, kr, vt, p, a_norm_w, *, batch, seq, lambda_init, tq=1024):
    nq = seq // tq
    tk = tq // 2
    z_col0 = 3584 // A_VDIM
    return pl.pallas_call(
        functools.partial(_attn_kernel, tq=tq, tk=tk, nq=nq, lambda_init=lambda_init),
        grid=(batch, A_HEADS),
        in_specs=[
            pl.BlockSpec((4, A_QK_DIM), lambda b, h: (0, 0)),
            pl.BlockSpec((A_VDIM, seq), lambda b, h: (b * A_HEADS + h, 0)),
            pl.BlockSpec((seq, A_VDIM), lambda b, h: (b, h)),
            pl.BlockSpec((A_VDIM, seq), lambda b, h: (b * A_HEADS + h, 0)),
            pl.BlockSpec((seq, A_VDIM), lambda b, h: (b, z_col0 + h)),
            pl.BlockSpec((1, A_VDIM), lambda b, h: (0, 0)),
        ],
        out_specs=pl.BlockSpec((seq, A_VDIM), lambda b, h: (b, h)),
        out_shape=jax.ShapeDtypeStruct((batch * seq, A_WIDTH), BF16),
        scratch_shapes=[
            pltpu.VMEM((1, 2 * tq), F32),
            pltpu.VMEM((A_VDIM + ONES_ROWS, 2 * tq), F32),
            pltpu.VMEM((A_VDIM, 2 * tq), BF16),
            pltpu.VMEM((tk, 2 * tq + SCORE_PAD), F32),
            pltpu.VMEM((tk, 2 * tq + SCORE_PAD), F32),
            pltpu.VMEM((1, 2 * tq), F32),
            pltpu.VMEM((1, 2 * tq), F32),
        ],
        compiler_params=pltpu.CompilerParams(
            dimension_semantics=("arbitrary", "arbitrary"),
            vmem_limit_bytes=VMEM_LIMIT),
        name="diffattn",
    )(lam_vec, qt, kr, vt, p, a_norm_w.reshape(1, A_VDIM))


def _outproj_kernel(ym_ref, ya_ref, w_ref, x_ref, fw_ref, o_ref, *, final):
    yc = jnp.concatenate([ym_ref[...], ya_ref[...]], axis=1)
    xn = x_ref[...] + jnp.dot(yc, w_ref[...], preferred_element_type=F32)
    if final:
        ms = jnp.mean(xn * xn, axis=-1, keepdims=True)
        xn = xn * lax.rsqrt(ms + EPS) * fw_ref[...]
    o_ref[...] = xn


def _outproj(ym, ya, w_out, x2, final_w, *, final, tm=1024):
    rows = x2.shape[0]
    return pl.pallas_call(
        functools.partial(_outproj_kernel, final=final),
        grid=(rows // tm,),
        in_specs=[
            pl.BlockSpec((tm, M_WIDTH), lambda i: (i, 0)),
            pl.BlockSpec((tm, A_WIDTH), lambda i: (i, 0)),
            pl.BlockSpec((D_MODEL, D_MODEL), lambda i: (0, 0)),
            pl.BlockSpec((tm, D_MODEL), lambda i: (i, 0)),
            pl.BlockSpec((1, D_MODEL), lambda i: (0, 0)),
        ],
        out_specs=pl.BlockSpec((tm, D_MODEL), lambda i: (i, 0)),
        out_shape=jax.ShapeDtypeStruct((rows, D_MODEL), F32),
        compiler_params=pltpu.CompilerParams(
            dimension_semantics=("arbitrary",), vmem_limit_bytes=VMEM_LIMIT),
        name="outproj",
    )(ym, ya, w_out, x2, final_w.reshape(1, D_MODEL))


def _gate_lanes(i_part, f_part):
    rows = i_part.shape[0]
    z = lambda n: jnp.zeros((rows, n), i_part.dtype)
    return jnp.concatenate([i_part, z(STAT_ROWS - M_HEADS), f_part,
                            z(GATE_LANES - STAT_ROWS - M_HEADS)], axis=1)


def _rope_tables(seq):
    dh = A_QK_DIM
    inv = 1.0 / (ROPE_THETA ** (jnp.arange(0, dh, 2, dtype=F32) / dh))
    hi = jnp.arange(seq // ROPE_BLOCK, dtype=F32)[:, None] * float(ROPE_BLOCK) * inv[None, :]
    lo = jnp.arange(ROPE_BLOCK, dtype=F32)[:, None] * inv[None, :]
    ch, sh = jnp.cos(hi)[:, None, :], jnp.sin(hi)[:, None, :]
    cl, sl = jnp.cos(lo)[None, :, :], jnp.sin(lo)[None, :, :]
    cos = (ch * cl - sh * sl).reshape(seq, dh // 2)
    sin = (sh * cl + ch * sl).reshape(seq, dh // 2)
    cos = jnp.concatenate([cos] * 4, axis=-1)
    sin = jnp.concatenate([-sin, sin, -sin, sin], axis=-1)
    return cos, sin


def kernel(x, norm_w, w_in, conv_w, conv_b, i_bias, f_bias, m_norm_w, m_skip,
           lam_q1, lam_k1, lam_q2, lam_k2, a_norm_w, w_out, final_norm_w):
    batch, seq, _ = x.shape
    depth = w_in.shape[0]
    x2 = x.reshape(batch * seq, D_MODEL)
    cos_t, sin_t = _rope_tables(seq)
    g0 = 3 * M_WIDTH
    g1 = g0 + 2 * M_HEADS
    for l in range(depth):
        lambda_init = 0.8 - 0.6 * math.exp(-0.3 * l)
        w_main = jnp.concatenate([w_in[l, :, :g0], w_in[l, :, g1:]], axis=1).astype(BF16)
        w_gate = _gate_lanes(w_in[l, :, g0:g0 + M_HEADS], w_in[l, :, g0 + M_HEADS:g1]).astype(BF16)
        gate_bias = _gate_lanes(i_bias[l][None, :], f_bias[l][None, :])
        lam_vec = jnp.stack([lam_q1[l], lam_k1[l], lam_q2[l], lam_k2[l]]).astype(F32)

        p, gates = _inproj(x2, norm_w[l], w_main, w_gate)
        ym = _mlstm(p, gates, conv_w[l], conv_b[l], gate_bias, m_norm_w[l], m_skip[l],
                    batch=batch, seq=seq)
        qt, kr, vt = _rope(p, cos_t, sin_t, batch=batch, seq=seq)
        ya = _attn(lam_vec, qt, kr, vt, p, a_norm_w[l], batch=batch, seq=seq,
                   lambda_init=lambda_init)
        x2 = _outproj(ym, ya, w_out[l].astype(BF16), x2, final_norm_w,
                      final=(l == depth - 1))
    return x2.reshape(batch, seq, D_MODEL)
```

```python
import functools
import math

import jax
import jax.numpy as jnp
from jax import lax
from jax.experimental import pallas as pl
from jax.experimental.pallas import tpu as pltpu

F32 = jnp.float32
BF16 = jnp.bfloat16

D_MODEL = 1024
M_WIDTH = 512
M_HEADS = 4
M_HEAD_DIM = 128
CONV_WIDTH = 4
A_WIDTH = 512
A_HEADS = 4
A_VDIM = 128
A_QK_DIM = 64
ATTN_CHUNK = 64
ROPE_THETA = 10000.0
ROPE_BLOCK = 64
EPS = 1e-6
LOG2E = 1.4426950408889634

P_WIDTH = 4096
GATE_LANES = 128
STAT_ROWS = 8

VMEM_LIMIT = 56 * 1024 * 1024
NEG_BIG = -1e30


def _silu(y):
    hy = 0.5 * y
    return hy + hy * jnp.tanh(hy)


def _inproj_kernel(x_ref, nw_ref, w_ref, wg_ref, p_ref, g_ref, h_scr):
    @pl.when(pl.program_id(1) == 0)
    def _():
        x = x_ref[...]
        ms = jnp.mean(x * x, axis=-1, keepdims=True)
        h = (x * lax.rsqrt(ms + EPS) * nw_ref[...]).astype(BF16)
        h_scr[...] = h
        g_ref[...] = jnp.dot(h, wg_ref[...], preferred_element_type=F32)

    p_ref[...] = jnp.dot(h_scr[...], w_ref[...], preferred_element_type=F32).astype(BF16)


def _inproj(x2, norm_w, w_main, w_gate, *, tm=1024, tn=2048):
    rows = x2.shape[0]
    return pl.pallas_call(
        _inproj_kernel,
        grid=(rows // tm, P_WIDTH // tn),
        in_specs=[
            pl.BlockSpec((tm, D_MODEL), lambda i, j: (i, 0)),
            pl.BlockSpec((1, D_MODEL), lambda i, j: (0, 0)),
            pl.BlockSpec((D_MODEL, tn), lambda i, j: (0, j)),
            pl.BlockSpec((D_MODEL, GATE_LANES), lambda i, j: (0, 0)),
        ],
        out_specs=[
            pl.BlockSpec((tm, tn), lambda i, j: (i, j)),
            pl.BlockSpec((tm, GATE_LANES), lambda i, j: (i, 0)),
        ],
        out_shape=[
            jax.ShapeDtypeStruct((rows, P_WIDTH), BF16),
            jax.ShapeDtypeStruct((rows, GATE_LANES), F32),
        ],
        scratch_shapes=[pltpu.VMEM((tm, D_MODEL), BF16)],
        compiler_params=pltpu.CompilerParams(
            dimension_semantics=("arbitrary", "arbitrary"),
            vmem_limit_bytes=VMEM_LIMIT),
        name="inproj",
    )(x2, norm_w.reshape(1, D_MODEL), w_main, w_gate)


def _dot_f32_exact_rhs01(x, tri):
    hi = x.astype(BF16)
    r1 = x - hi.astype(F32)
    mid = r1.astype(BF16)
    lo = (r1 - mid.astype(F32)).astype(BF16)
    d = lambda t: jnp.dot(t, tri, preferred_element_type=F32)
    return d(hi) + d(mid) + d(lo)


def _mlstm_kernel(qk_ref, v_ref, z_ref, g_ref, cw_ref, cb_ref, gb_ref, nw_ref, sk_ref,
                  o_ref, xbuf, qkc, hm_scr, c_scr, m_scr, *, L, batch):
    n = pl.program_id(0)
    D = M_HEAD_DIM
    R = STAT_ROWS
    bs = range(batch)

    @pl.when(n == 0)
    def _():
        xbuf[:, 0:8, :] = jnp.zeros((batch, 8, 2 * M_WIDTH), F32)
        c_scr[...] = jnp.zeros(c_scr.shape, F32)
        m_scr[...] = jnp.zeros(m_scr.shape, F32)

    row = lax.broadcasted_iota(jnp.int32, (L, L), 0)
    col = lax.broadcasted_iota(jnp.int32, (L, L), 1)
    causal = col <= row
    triu = jnp.where(row <= col, 1.0, 0.0).astype(BF16)
    lane = lax.broadcasted_iota(jnp.int32, (R, L), 1)

    def gate_stats(bi):
        gt = (g_ref[bi] + gb_ref[...]).T
        li = gt[0:R, :]
        fp = gt[R:2 * R, :]
        lf = jnp.minimum(fp, 0.0) - jnp.log1p(jnp.exp(-jnp.abs(fp)))
        b = _dot_f32_exact_rhs01(lf, triu)
        a = li - b
        pm = a
        shift = 1
        while shift < L:
            pm = jnp.maximum(pm, jnp.where(lane >= shift, pltpu.roll(pm, shift, 1), NEG_BIG))
            shift *= 2
        m_old = m_scr[bi]
        m_prev = jnp.concatenate([m_old] * (L // m_old.shape[1]), axis=1)
        m_i = jnp.maximum(b + m_prev, b + pm)
        w_inter = jnp.exp(b + m_prev - m_i)
        e_neg = jnp.exp(-m_i)
        b_last = b[:, L - 1:L]
        m_new = b_last + jnp.maximum(m_old[:, 0:1], pm[:, L - 1:L])
        decay = jnp.exp(b_last + m_old[:, 0:1] - m_new)
        wk = jnp.exp(b_last + a - m_new)
        m_scr[bi] = jnp.broadcast_to(m_new, m_old.shape)
        cols = jnp.concatenate([(m_i - b) * LOG2E, w_inter, e_neg, wk,
                                jnp.zeros((GATE_LANES - 4 * R, L), F32)], axis=0).T
        return a * LOG2E, cols, decay

    stats = []
    for bi in bs:
        stats.append(gate_stats(bi))
        xb = xbuf.at[bi]
        xb[8:8 + L, :] = qk_ref[bi].astype(F32)
        y = cb_ref[...] + cw_ref[3:4, :] * xb[8:8 + L, :]
        y = y + cw_ref[2:3, :] * xb[7:7 + L, :]
        y = y + cw_ref[1:2, :] * xb[6:6 + L, :]
        y = y + cw_ref[0:1, :] * xb[5:5 + L, :]
        xb[0:8, :] = xb[L:L + 8, :]
        qkc[bi] = _silu(y)

    ones = jnp.ones((L, D), BF16)
    for h in range(M_HEADS):
        sl = slice(h * D, (h + 1) * D)
        for bi in bs:
            a2, cols, decay = stats[bi]
            c_col = cols[:, h:h + 1]
            wi_col = cols[:, R + h:R + h + 1]
            en_col = cols[:, 2 * R + h:2 * R + h + 1]
            wk_col = cols[:, 3 * R + h:3 * R + h + 1]
            w_intra = jnp.where(causal, jnp.exp2(a2[h:h + 1, :] - c_col), 0.0)

            qb = qkc[bi, :, sl].astype(BF16)
            kh = qkc[bi, :, M_WIDTH + h * D:M_WIDTH + (h + 1) * D] * (D ** -0.5)
            v_ext = jnp.concatenate([v_ref[bi, :, sl], ones], axis=1)
            s = lax.dot_general(qb, kh.astype(BF16), (((1,), (1,)), ((), ())),
                                preferred_element_type=F32) * w_intra
            c_old = c_scr[bi, h]
            q_c = jnp.dot(qb, c_old.astype(BF16), preferred_element_type=F32)
            s_v = jnp.dot(s.astype(BF16), v_ext, preferred_element_type=F32)
            num = wi_col * q_c[:, :D] + s_v[:, :D]
            den = wi_col * q_c[:, D:D + 1] + s_v[:, D:D + 1]
            hm_scr[bi, :, sl] = num * (1.0 / jnp.maximum(jnp.abs(den), en_col))

            kw_t = (kh * wk_col).T.astype(BF16)
            c_scr[bi, h] = decay[h:h + 1, :] * c_old + jnp.dot(kw_t, v_ext,
                                                               preferred_element_type=F32)

    for h in range(M_HEADS):
        sl = slice(h * D, (h + 1) * D)
        for bi in bs:
            hm = hm_scr[bi, :, sl]
            ms = jnp.mean(hm * hm, axis=-1, keepdims=True)
            hn = hm * lax.rsqrt(ms + EPS) * nw_ref[:, sl]
            ym = (hn + sk_ref[:, sl] * qkc[bi, :, sl]) * _silu(z_ref[bi, :, sl].astype(F32))
            o_ref[bi, :, sl] = ym.astype(BF16)


def _mlstm(p, gates, conv_w, conv_b, gate_bias, m_norm_w, m_skip, *, batch, seq, L=256):
    nchunk = seq // L
    p3 = p.reshape(batch, seq, P_WIDTH)
    g3 = gates.reshape(batch, seq, GATE_LANES)
    const = lambda n: (0, 0)
    ym = pl.pallas_call(
        functools.partial(_mlstm_kernel, L=L, batch=batch),
        grid=(nchunk,),
        in_specs=[
            pl.BlockSpec((batch, L, 2 * M_WIDTH), lambda n: (0, n, 0)),
            pl.BlockSpec((batch, L, M_WIDTH), lambda n: (0, n, 2)),
            pl.BlockSpec((batch, L, M_WIDTH), lambda n: (0, n, 3)),
            pl.BlockSpec((batch, L, GATE_LANES), lambda n: (0, n, 0)),
            pl.BlockSpec((CONV_WIDTH, 2 * M_WIDTH), const),
            pl.BlockSpec((1, 2 * M_WIDTH), const),
            pl.BlockSpec((1, GATE_LANES), const),
            pl.BlockSpec((1, M_WIDTH), const),
            pl.BlockSpec((1, M_WIDTH), const),
        ],
        out_specs=pl.BlockSpec((batch, L, M_WIDTH), lambda n: (0, n, 0)),
        out_shape=jax.ShapeDtypeStruct((batch, seq, M_WIDTH), BF16),
        scratch_shapes=[
            pltpu.VMEM((batch, L + 8, 2 * M_WIDTH), F32),
            pltpu.VMEM((batch, L, 2 * M_WIDTH), F32),
            pltpu.VMEM((batch, L, M_WIDTH), F32),
            pltpu.VMEM((batch, M_HEADS, M_HEAD_DIM, 2 * M_HEAD_DIM), F32),
            pltpu.VMEM((batch, STAT_ROWS, 128), F32),
        ],
        compiler_params=pltpu.CompilerParams(
            dimension_semantics=("arbitrary",),
            vmem_limit_bytes=VMEM_LIMIT),
        name="mlstm",
    )(p3, p3, p3, g3, conv_w, conv_b.reshape(1, -1), gate_bias,
      m_norm_w.reshape(1, -1), m_skip.reshape(1, -1))
    return ym.reshape(batch * seq, M_WIDTH)


def _rope_kernel(q_ref, k_ref, v_ref, cos_ref, sin_ref, qt_ref, ko_ref, vt_ref, *, q_scale):
    cos = jnp.concatenate([cos_ref[...]] * A_HEADS, axis=1)
    sin = jnp.concatenate([sin_ref[...]] * A_HEADS, axis=1)
    lane = lax.broadcasted_iota(jnp.int32, cos.shape, 1)
    first_half = (lane & (A_QK_DIM - 1)) < (A_QK_DIM // 2)

    def rot(x):
        fwd = pltpu.roll(x, A_QK_DIM // 2, 1)
        bwd = pltpu.roll(x, A_WIDTH - A_QK_DIM // 2, 1)
        return x * cos + jnp.where(first_half, bwd, fwd) * sin

    qt_ref[...] = (rot(q_ref[...].astype(F32)) * q_scale).T.astype(BF16)
    ko_ref[...] = rot(k_ref[...].astype(F32)).astype(BF16)
    vt_ref[...] = v_ref[...].astype(F32).T.astype(BF16)


def _rope(p, cos_t, sin_t, *, batch, seq, tm=1024):
    rows = p.shape[0]
    nsb = seq // tm
    q_scale = (A_QK_DIM ** -0.5) * LOG2E
    nat = jax.ShapeDtypeStruct((rows, A_WIDTH), BF16)
    tr = jax.ShapeDtypeStruct((batch * A_WIDTH, seq), BF16)
    tr_spec = pl.BlockSpec((A_WIDTH, tm), lambda i: (i // nsb, i % nsb))
    return pl.pallas_call(
        functools.partial(_rope_kernel, q_scale=q_scale),
        grid=(rows // tm,),
        in_specs=[
            pl.BlockSpec((tm, A_WIDTH), lambda i: (i, 4)),
            pl.BlockSpec((tm, A_WIDTH), lambda i: (i, 5)),
            pl.BlockSpec((tm, A_WIDTH), lambda i: (i, 6)),
            pl.BlockSpec((tm, 2 * A_QK_DIM), lambda i: (i % nsb, 0)),
            pl.BlockSpec((tm, 2 * A_QK_DIM), lambda i: (i % nsb, 0)),
        ],
        out_specs=[tr_spec, pl.BlockSpec((tm, A_WIDTH), lambda i: (i, 0)), tr_spec],
        out_shape=[tr, nat, tr],
        compiler_params=pltpu.CompilerParams(
            dimension_semantics=("arbitrary",), vmem_limit_bytes=VMEM_LIMIT),
        name="rope",
    )(p, p, p, cos_t, sin_t)


ONES_ROWS = 16
QGROUP = 256
SCORE_PAD = 128


def _attn_kernel(lam_ref, qt_ref, k_ref, vt_ref, z_ref, nw_ref, o_ref,
                 m_scr, acc_scr, q2_scr, sa, sb, cma, cmb, *, tq, tk, nq, lambda_init):
    dv = A_VDIM
    ng = 2 * tq // QGROUP
    ones = jnp.ones((ONES_ROWS, tk), BF16)

    def load_queries(i):
        qt = qt_ref[:, pl.ds(pl.multiple_of(i * tq, tq), tq)]
        rowi = lax.broadcasted_iota(jnp.int32, qt.shape, 0)
        zero = jnp.zeros_like(qt)
        q2_scr[:, 0:tq] = jnp.where(rowi < A_QK_DIM, qt, zero)
        q2_scr[:, tq:] = jnp.where(rowi >= A_QK_DIM, qt, zero)

    def reset_stats():
        m_scr[...] = jnp.full(m_scr.shape, NEG_BIG, F32)
        acc_scr[...] = jnp.zeros(acc_scr.shape, F32)

    def all_masked(diag_off, g):
        return diag_off is not None and diag_off >= (g * QGROUP) % tq + QGROUP

    def scores_group(start, s_ref, cm_ref, diag_off, g):
        gs = slice(g * QGROUP, (g + 1) * QGROUP)
        qlo = (g * QGROUP) % tq
        if all_masked(diag_off, g):
            return
        k = k_ref[pl.ds(start, tk), :]
        st = jnp.dot(k, q2_scr[:, gs], preferred_element_type=F32)
        if diag_off is not None and diag_off + tk > qlo + ATTN_CHUNK:
            kk = lax.broadcasted_iota(jnp.int32, st.shape, 0) + diag_off
            qq = lax.broadcasted_iota(jnp.int32, st.shape, 1) + qlo
            st = jnp.where((kk // ATTN_CHUNK) <= (qq // ATTN_CHUNK), st, NEG_BIG)
        s_ref[:, gs] = st
        cm_ref[:, gs] = jnp.max(st, axis=0, keepdims=True)

    def consume_group(start, s_ref, cm_ref, diag_off, g):
        gs = slice(g * QGROUP, (g + 1) * QGROUP)
        if all_masked(diag_off, g):
            return
        vt_ext = jnp.concatenate([vt_ref[:, pl.ds(start, tk)], ones], axis=0)
        m_old = m_scr[:, gs]
        m_new = jnp.maximum(m_old, cm_ref[:, gs])
        alpha = jnp.exp2(m_old - m_new)
        pt = jnp.exp2((s_ref[:, gs] - m_new).astype(BF16))
        acc_scr[:, gs] = alpha * acc_scr[:, gs] + jnp.dot(vt_ext, pt,
                                                          preferred_element_type=F32)
        m_scr[:, gs] = m_new

    def stage(score_args, consume_args):
        for g in range(ng):
            if score_args is not None:
                scores_group(*score_args, g)
            if consume_args is not None:
                consume_group(*consume_args, g)

    lv = lam_ref[...]
    lam = (jnp.exp(jnp.sum(lv[0:1, :] * lv[1:2, :], axis=1, keepdims=True))
           - jnp.exp(jnp.sum(lv[2:3, :] * lv[3:4, :], axis=1, keepdims=True))
           + lambda_init)

    def finalize(i):
        rows = pl.ds(pl.multiple_of(i * tq, tq), tq)
        o0 = acc_scr[0:dv, 0:tq] / acc_scr[dv:dv + 1, 0:tq]
        o1 = acc_scr[0:dv, tq:2 * tq] / acc_scr[dv:dv + 1, tq:2 * tq]
        out_t = o0 - lam * o1
        ms = jnp.mean(out_t * out_t, axis=0, keepdims=True)
        hn = (out_t * lax.rsqrt(ms + EPS)).T
        hn = hn * nw_ref[...] * (1.0 - lambda_init)
        o_ref[rows, :] = (hn * _silu(z_ref[rows, :].astype(F32))).astype(BF16)

    def off(x):
        return pl.multiple_of(x, tk)

    def diag_tail(i):
        d0 = i * tq
        stage((off(d0 + tk), sb, cmb, tk), (off(d0), sa, cma, 0))
        load_queries(jnp.minimum(i + 1, nq - 1))
        stage((0, sa, cma, None), (off(d0 + tk), sb, cmb, tk))
        finalize(i)
        reset_stats()

    reset_stats()
    load_queries(0)
    stage((0, sa, cma, 0), None)
    diag_tail(0)

    def q_block(i, carry):
        def two_steps(t):
            u = 2 * t * tk
            stage((off(u + tk), sb, cmb, None), (off(u), sa, cma, None))
            stage((off(u + 2 * tk), sa, cma, None), (off(u + tk), sb, cmb, None))

        def body(t, c):
            two_steps(2 * t)
            two_steps(2 * t + 1)
            return c

        n_two = i - 1
        lax.fori_loop(0, lax.shift_right_logical(n_two, 1), body, 0)

        @pl.when((n_two & 1) == 1)
        def _():
            two_steps(n_two - 1)

        u = (2 * i - 2) * tk
        stage((off(u + tk), sb, cmb, None), (off(u), sa, cma, None))
        stage((off(i * tq), sa, cma, 0), (off(u + tk), sb, cmb, None))
        diag_tail(i)
        return carry

    lax.fori_loop(1, nq, q_block, 0)


def _attn(lam_vec, qt, kr, vt, p, a_norm_w, *, batch, seq, lambda_init, tq=1024):
    nq = seq // tq
    tk = tq // 2
    z_col0 = 3584 // A_VDIM
    return pl.pallas_call(
        functools.partial(_attn_kernel, tq=tq, tk=tk, nq=nq, lambda_init=lambda_init),
        grid=(batch, A_HEADS),
        in_specs=[
            pl.BlockSpec((4, A_QK_DIM), lambda b, h: (0, 0)),
            pl.BlockSpec((A_VDIM, seq), lambda b, h: (b * A_HEADS + h, 0)),
            pl.BlockSpec((seq, A_VDIM), lambda b, h: (b, h)),
            pl.BlockSpec((A_VDIM, seq), lambda b, h: (b * A_HEADS + h, 0)),
            pl.BlockSpec((seq, A_VDIM), lambda b, h: (b, z_col0 + h)),
            pl.BlockSpec((1, A_VDIM), lambda b, h: (0, 0)),
        ],
        out_specs=pl.BlockSpec((seq, A_VDIM), lambda b, h: (b, h)),
        out_shape=jax.ShapeDtypeStruct((batch * seq, A_WIDTH), BF16),
        scratch_shapes=[
            pltpu.VMEM((1, 2 * tq), F32),
            pltpu.VMEM((A_VDIM + ONES_ROWS, 2 * tq + SCORE_PAD), F32),
            pltpu.VMEM((A_VDIM, 2 * tq), BF16),
            pltpu.VMEM((tk, 2 * tq + SCORE_PAD), F32),
            pltpu.VMEM((tk, 2 * tq + SCORE_PAD), F32),
            pltpu.VMEM((1, 2 * tq), F32),
            pltpu.VMEM((1, 2 * tq), F32),
        ],
        compiler_params=pltpu.CompilerParams(
            dimension_semantics=("arbitrary", "arbitrary"),
            vmem_limit_bytes=VMEM_LIMIT),
        name="diffattn",
    )(lam_vec, qt, kr, vt, p, a_norm_w.reshape(1, A_VDIM))


def _outproj_kernel(ym_ref, ya_ref, w_ref, x_ref, fw_ref, o_ref, *, final):
    yc = jnp.concatenate([ym_ref[...], ya_ref[...]], axis=1)
    xn = x_ref[...] + jnp.dot(yc, w_ref[...], preferred_element_type=F32)
    if final:
        ms = jnp.mean(xn * xn, axis=-1, keepdims=True)
        xn = xn * lax.rsqrt(ms + EPS) * fw_ref[...]
    o_ref[...] = xn


def _outproj(ym, ya, w_out, x2, final_w, *, final, tm=1024):
    rows = x2.shape[0]
    return pl.pallas_call(
        functools.partial(_outproj_kernel, final=final),
        grid=(rows // tm,),
        in_specs=[
            pl.BlockSpec((tm, M_WIDTH), lambda i: (i, 0)),
            pl.BlockSpec((tm, A_WIDTH), lambda i: (i, 0)),
            pl.BlockSpec((D_MODEL, D_MODEL), lambda i: (0, 0)),
            pl.BlockSpec((tm, D_MODEL), lambda i: (i, 0)),
            pl.BlockSpec((1, D_MODEL), lambda i: (0, 0)),
        ],
        out_specs=pl.BlockSpec((tm, D_MODEL), lambda i: (i, 0)),
        out_shape=jax.ShapeDtypeStruct((rows, D_MODEL), F32),
        compiler_params=pltpu.CompilerParams(
            dimension_semantics=("arbitrary",), vmem_limit_bytes=VMEM_LIMIT),
        name="outproj",
    )(ym, ya, w_out, x2, final_w.reshape(1, D_MODEL))


def _gate_lanes(i_part, f_part):
    rows = i_part.shape[0]
    z = lambda n: jnp.zeros((rows, n), i_part.dtype)
    return jnp.concatenate([i_part, z(STAT_ROWS - M_HEADS), f_part,
                            z(GATE_LANES - STAT_ROWS - M_HEADS)], axis=1)


def _rope_tables(seq):
    dh = A_QK_DIM
    inv = 1.0 / (ROPE_THETA ** (jnp.arange(0, dh, 2, dtype=F32) / dh))
    hi = jnp.arange(seq // ROPE_BLOCK, dtype=F32)[:, None] * float(ROPE_BLOCK) * inv[None, :]
    lo = jnp.arange(ROPE_BLOCK, dtype=F32)[:, None] * inv[None, :]
    ch, sh = jnp.cos(hi)[:, None, :], jnp.sin(hi)[:, None, :]
    cl, sl = jnp.cos(lo)[None, :, :], jnp.sin(lo)[None, :, :]
    cos = (ch * cl - sh * sl).reshape(seq, dh // 2)
    sin = (sh * cl + ch * sl).reshape(seq, dh // 2)
    cos = jnp.concatenate([cos] * 4, axis=-1)
    sin = jnp.concatenate([-sin, sin, -sin, sin], axis=-1)
    return cos, sin


def kernel(x, norm_w, w_in, conv_w, conv_b, i_bias, f_bias, m_norm_w, m_skip,
           lam_q1, lam_k1, lam_q2, lam_k2, a_norm_w, w_out, final_norm_w):
    batch, seq, _ = x.shape
    depth = w_in.shape[0]
    x2 = x.reshape(batch * seq, D_MODEL)
    cos_t, sin_t = _rope_tables(seq)
    g0 = 3 * M_WIDTH
    g1 = g0 + 2 * M_HEADS
    for l in range(depth):
        lambda_init = 0.8 - 0.6 * math.exp(-0.3 * l)
        w_main = jnp.concatenate([w_in[l, :, :g0], w_in[l, :, g1:]], axis=1).astype(BF16)
        w_gate = _gate_lanes(w_in[l, :, g0:g0 + M_HEADS], w_in[l, :, g0 + M_HEADS:g1]).astype(BF16)
        gate_bias = _gate_lanes(i_bias[l][None, :], f_bias[l][None, :])
        lam_vec = jnp.stack([lam_q1[l], lam_k1[l], lam_q2[l], lam_k2[l]]).astype(F32)

        p, gates = _inproj(x2, norm_w[l], w_main, w_gate)
        ym = _mlstm(p, gates, conv_w[l], conv_b[l], gate_bias, m_norm_w[l], m_skip[l],
                    batch=batch, seq=seq)
        qt, kr, vt = _rope(p, cos_t, sin_t, batch=batch, seq=seq)
        ya = _attn(lam_vec, qt, kr, vt, p, a_norm_w[l], batch=batch, seq=seq,
                   lambda_init=lambda_init)
        x2 = _outproj(ym, ya, w_out[l].astype(BF16), x2, final_norm_w,
                      final=(l == depth - 1))
    return x2.reshape(batch, seq, D_MODEL)
```

```python
import functools
import math

import jax
import jax.numpy as jnp
from jax import lax
from jax.experimental import pallas as pl
from jax.experimental.pallas import tpu as pltpu

F32 = jnp.float32
BF16 = jnp.bfloat16

D_MODEL = 1024
M_WIDTH = 512
M_HEADS = 4
M_HEAD_DIM = 128
CONV_WIDTH = 4
A_WIDTH = 512
A_HEADS = 4
A_VDIM = 128
A_QK_DIM = 64
ATTN_CHUNK = 64
ROPE_THETA = 10000.0
ROPE_BLOCK = 64
EPS = 1e-6
LOG2E = 1.4426950408889634

P_WIDTH = 4096
GATE_LANES = 128
STAT_ROWS = 8

VMEM_LIMIT = 56 * 1024 * 1024
NEG_BIG = -1e30


def _silu(y):
    hy = 0.5 * y
    return hy + hy * jnp.tanh(hy)


def _inproj_kernel(x_ref, nw_ref, w_ref, wg_ref, p_ref, g_ref, h_scr):
    @pl.when(pl.program_id(1) == 0)
    def _():
        x = x_ref[...]
        ms = jnp.mean(x * x, axis=-1, keepdims=True)
        h = (x * lax.rsqrt(ms + EPS) * nw_ref[...]).astype(BF16)
        h_scr[...] = h
        g_ref[...] = jnp.dot(h, wg_ref[...], preferred_element_type=F32)

    p_ref[...] = jnp.dot(h_scr[...], w_ref[...], preferred_element_type=F32).astype(BF16)


def _inproj(x2, norm_w, w_main, w_gate, *, tm=1024, tn=2048):
    rows = x2.shape[0]
    return pl.pallas_call(
        _inproj_kernel,
        grid=(rows // tm, P_WIDTH // tn),
        in_specs=[
            pl.BlockSpec((tm, D_MODEL), lambda i, j: (i, 0)),
            pl.BlockSpec((1, D_MODEL), lambda i, j: (0, 0)),
            pl.BlockSpec((D_MODEL, tn), lambda i, j: (0, j)),
            pl.BlockSpec((D_MODEL, GATE_LANES), lambda i, j: (0, 0)),
        ],
        out_specs=[
            pl.BlockSpec((tm, tn), lambda i, j: (i, j)),
            pl.BlockSpec((tm, GATE_LANES), lambda i, j: (i, 0)),
        ],
        out_shape=[
            jax.ShapeDtypeStruct((rows, P_WIDTH), BF16),
            jax.ShapeDtypeStruct((rows, GATE_LANES), F32),
        ],
        scratch_shapes=[pltpu.VMEM((tm, D_MODEL), BF16)],
        compiler_params=pltpu.CompilerParams(
            dimension_semantics=("arbitrary", "arbitrary"),
            vmem_limit_bytes=VMEM_LIMIT),
        name="inproj",
    )(x2, norm_w.reshape(1, D_MODEL), w_main, w_gate)


def _dot_f32_exact_rhs01(x, tri):
    hi = x.astype(BF16)
    r1 = x - hi.astype(F32)
    mid = r1.astype(BF16)
    lo = (r1 - mid.astype(F32)).astype(BF16)
    d = lambda t: jnp.dot(t, tri, preferred_element_type=F32)
    return d(hi) + d(mid) + d(lo)


def _mlstm_kernel(qk_ref, v_ref, z_ref, g_ref, cw_ref, cb_ref, gb_ref, nw_ref, sk_ref,
                  o_ref, xbuf, qkc, hm_scr, c_scr, m_scr, *, L, batch):
    n = pl.program_id(0)
    D = M_HEAD_DIM
    R = STAT_ROWS
    bs = range(batch)

    @pl.when(n == 0)
    def _():
        xbuf[:, 0:8, :] = jnp.zeros((batch, 8, 2 * M_WIDTH), F32)
        c_scr[...] = jnp.zeros(c_scr.shape, F32)
        m_scr[...] = jnp.zeros(m_scr.shape, F32)

    row = lax.broadcasted_iota(jnp.int32, (L, L), 0)
    col = lax.broadcasted_iota(jnp.int32, (L, L), 1)
    causal = col <= row
    triu = jnp.where(row <= col, 1.0, 0.0).astype(BF16)
    lane = lax.broadcasted_iota(jnp.int32, (R, L), 1)

    def gate_stats(bi):
        gt = (g_ref[bi] + gb_ref[...]).T
        li = gt[0:R, :]
        fp = gt[R:2 * R, :]
        lf = jnp.minimum(fp, 0.0) - jnp.log1p(jnp.exp(-jnp.abs(fp)))
        b = _dot_f32_exact_rhs01(lf, triu)
        a = li - b
        pm = a
        shift = 1
        while shift < L:
            pm = jnp.maximum(pm, jnp.where(lane >= shift, pltpu.roll(pm, shift, 1), NEG_BIG))
            shift *= 2
        m_old = m_scr[bi]
        m_prev = jnp.concatenate([m_old] * (L // m_old.shape[1]), axis=1)
        m_i = jnp.maximum(b + m_prev, b + pm)
        w_inter = jnp.exp(b + m_prev - m_i)
        e_neg = jnp.exp(-m_i)
        b_last = b[:, L - 1:L]
        m_new = b_last + jnp.maximum(m_old[:, 0:1], pm[:, L - 1:L])
        decay = jnp.exp(b_last + m_old[:, 0:1] - m_new)
        wk = jnp.exp(b_last + a - m_new)
        m_scr[bi] = jnp.broadcast_to(m_new, m_old.shape)
        cols = jnp.concatenate([(m_i - b) * LOG2E, w_inter, e_neg, wk,
                                jnp.zeros((GATE_LANES - 4 * R, L), F32)], axis=0).T
        return a * LOG2E, cols, decay

    stats = []
    for bi in bs:
        stats.append(gate_stats(bi))
        xb = xbuf.at[bi]
        xb[8:8 + L, :] = qk_ref[bi].astype(F32)
        y = cb_ref[...] + cw_ref[3:4, :] * xb[8:8 + L, :]
        y = y + cw_ref[2:3, :] * xb[7:7 + L, :]
        y = y + cw_ref[1:2, :] * xb[6:6 + L, :]
        y = y + cw_ref[0:1, :] * xb[5:5 + L, :]
        xb[0:8, :] = xb[L:L + 8, :]
        qkc[bi] = _silu(y)

    ones = jnp.ones((L, D), BF16)
    for h in range(M_HEADS):
        sl = slice(h * D, (h + 1) * D)
        for bi in bs:
            a2, cols, decay = stats[bi]
            c_col = cols[:, h:h + 1]
            wi_col = cols[:, R + h:R + h + 1]
            en_col = cols[:, 2 * R + h:2 * R + h + 1]
            wk_col = cols[:, 3 * R + h:3 * R + h + 1]
            w_intra = jnp.where(causal, jnp.exp2(a2[h:h + 1, :] - c_col), 0.0)

            qb = qkc[bi, :, sl].astype(BF16)
            kh = qkc[bi, :, M_WIDTH + h * D:M_WIDTH + (h + 1) * D] * (D ** -0.5)
            v_ext = jnp.concatenate([v_ref[bi, :, sl], ones], axis=1)
            s = lax.dot_general(qb, kh.astype(BF16), (((1,), (1,)), ((), ())),
                                preferred_element_type=F32) * w_intra
            c_old = c_scr[bi, h]
            q_c = jnp.dot(qb, c_old.astype(BF16), preferred_element_type=F32)
            s_v = jnp.dot(s.astype(BF16), v_ext, preferred_element_type=F32)
            num = wi_col * q_c[:, :D] + s_v[:, :D]
            den = wi_col * q_c[:, D:D + 1] + s_v[:, D:D + 1]
            hm_scr[bi, :, sl] = num * (1.0 / jnp.maximum(jnp.abs(den), en_col))

            kw_t = (kh * wk_col).T.astype(BF16)
            c_scr[bi, h] = decay[h:h + 1, :] * c_old + jnp.dot(kw_t, v_ext,
                                                               preferred_element_type=F32)

    for h in range(M_HEADS):
        sl = slice(h * D, (h + 1) * D)
        for bi in bs:
            hm = hm_scr[bi, :, sl]
            ms = jnp.mean(hm * hm, axis=-1, keepdims=True)
            hn = hm * lax.rsqrt(ms + EPS) * nw_ref[:, sl]
            ym = (hn + sk_ref[:, sl] * qkc[bi, :, sl]) * _silu(z_ref[bi, :, sl].astype(F32))
            o_ref[bi, :, sl] = ym.astype(BF16)


def _mlstm(p, gates, conv_w, conv_b, gate_bias, m_norm_w, m_skip, *, batch, seq, L=256):
    nchunk = seq // L
    p3 = p.reshape(batch, seq, P_WIDTH)
    g3 = gates.reshape(batch, seq, GATE_LANES)
    const = lambda n: (0, 0)
    ym = pl.pallas_call(
        functools.partial(_mlstm_kernel, L=L, batch=batch),
        grid=(nchunk,),
        in_specs=[
            pl.BlockSpec((batch, L, 2 * M_WIDTH), lambda n: (0, n, 0)),
            pl.BlockSpec((batch, L, M_WIDTH), lambda n: (0, n, 2)),
            pl.BlockSpec((batch, L, M_WIDTH), lambda n: (0, n, 3)),
            pl.BlockSpec((batch, L, GATE_LANES), lambda n: (0, n, 0)),
            pl.BlockSpec((CONV_WIDTH, 2 * M_WIDTH), const),
            pl.BlockSpec((1, 2 * M_WIDTH), const),
            pl.BlockSpec((1, GATE_LANES), const),
            pl.BlockSpec((1, M_WIDTH), const),
            pl.BlockSpec((1, M_WIDTH), const),
        ],
        out_specs=pl.BlockSpec((batch, L, M_WIDTH), lambda n: (0, n, 0)),
        out_shape=jax.ShapeDtypeStruct((batch, seq, M_WIDTH), BF16),
        scratch_shapes=[
            pltpu.VMEM((batch, L + 8, 2 * M_WIDTH), F32),
            pltpu.VMEM((batch, L, 2 * M_WIDTH), F32),
            pltpu.VMEM((batch, L, M_WIDTH), F32),
            pltpu.VMEM((batch, M_HEADS, M_HEAD_DIM, 2 * M_HEAD_DIM), F32),
            pltpu.VMEM((batch, STAT_ROWS, 128), F32),
        ],
        compiler_params=pltpu.CompilerParams(
            dimension_semantics=("arbitrary",),
            vmem_limit_bytes=VMEM_LIMIT),
        name="mlstm",
    )(p3, p3, p3, g3, conv_w, conv_b.reshape(1, -1), gate_bias,
      m_norm_w.reshape(1, -1), m_skip.reshape(1, -1))
    return ym.reshape(batch * seq, M_WIDTH)


def _rope_kernel(q_ref, k_ref, v_ref, cos_ref, sin_ref, cost_ref, sint_ref, rot_ref, rott_ref,
                 eye_ref, qt_ref, ko_ref, vt_ref):
    nt = (((1,), (1,)), ((), ()))
    hd = 2 * A_QK_DIM
    for h in range(A_HEADS):
        sl = slice(h * hd, (h + 1) * hd)
        xq = q_ref[:, sl]
        xq_t = lax.dot_general(eye_ref[...], xq, nt, preferred_element_type=F32)
        rq_t = lax.dot_general(rott_ref[...], xq, nt, preferred_element_type=F32)
        qt_ref[sl, :] = (xq_t * cost_ref[...] + rq_t * sint_ref[...]).astype(BF16)
        xk = k_ref[:, sl]
        rk = jnp.dot(xk, rot_ref[...], preferred_element_type=F32)
        ko_ref[:, sl] = (xk.astype(F32) * cos_ref[...] + rk * sin_ref[...]).astype(BF16)
        vt_ref[sl, :] = lax.dot_general(eye_ref[...], v_ref[:, sl], nt,
                                        preferred_element_type=F32).astype(BF16)


def _rope(p, tables, *, batch, seq, tm=1024):
    cos_t, sin_t, cos_tt, sin_tt, rot, eye = tables
    rows = p.shape[0]
    nsb = seq // tm
    hd = 2 * A_QK_DIM
    nat = jax.ShapeDtypeStruct((rows, A_WIDTH), BF16)
    tr = jax.ShapeDtypeStruct((batch * A_WIDTH, seq), BF16)
    tr_spec = pl.BlockSpec((A_WIDTH, tm), lambda i: (i // nsb, i % nsb))
    tab = pl.BlockSpec((tm, hd), lambda i: (i % nsb, 0))
    tab_t = pl.BlockSpec((hd, tm), lambda i: (0, i % nsb))
    mat = pl.BlockSpec((hd, hd), lambda i: (0, 0))
    return pl.pallas_call(
        _rope_kernel,
        grid=(rows // tm,),
        in_specs=[
            pl.BlockSpec((tm, A_WIDTH), lambda i: (i, 4)),
            pl.BlockSpec((tm, A_WIDTH), lambda i: (i, 5)),
            pl.BlockSpec((tm, A_WIDTH), lambda i: (i, 6)),
            tab, tab, tab_t, tab_t, mat, mat, mat,
        ],
        out_specs=[tr_spec, pl.BlockSpec((tm, A_WIDTH), lambda i: (i, 0)), tr_spec],
        out_shape=[tr, nat, tr],
        compiler_params=pltpu.CompilerParams(
            dimension_semantics=("arbitrary",), vmem_limit_bytes=VMEM_LIMIT),
        name="rope",
    )(p, p, p, cos_t, sin_t, cos_tt, sin_tt, rot, rot.T, eye)


ONES_ROWS = 16
QGROUP = 256
SCORE_PAD = 128


def _attn_kernel(lam_ref, qt_ref, k_ref, vt_ref, z_ref, nw_ref, o_ref,
                 m_scr, acc_scr, q2_scr, sa, sb, cma, cmb, *, tq, tk, nq, lambda_init):
    dv = A_VDIM
    ng = 2 * tq // QGROUP
    ones = jnp.ones((ONES_ROWS, tk), BF16)

    def load_queries(i):
        qt = qt_ref[:, pl.ds(pl.multiple_of(i * tq, tq), tq)]
        rowi = lax.broadcasted_iota(jnp.int32, qt.shape, 0)
        zero = jnp.zeros_like(qt)
        q2_scr[:, 0:tq] = jnp.where(rowi < A_QK_DIM, qt, zero)
        q2_scr[:, tq:] = jnp.where(rowi >= A_QK_DIM, qt, zero)

    def reset_stats():
        m_scr[...] = jnp.full(m_scr.shape, NEG_BIG, F32)
        acc_scr[...] = jnp.zeros(acc_scr.shape, F32)

    def all_masked(diag_off, g):
        return diag_off is not None and diag_off >= (g * QGROUP) % tq + QGROUP

    def scores_group(start, s_ref, cm_ref, diag_off, g):
        gs = slice(g * QGROUP, (g + 1) * QGROUP)
        qlo = (g * QGROUP) % tq
        if all_masked(diag_off, g):
            return
        k = k_ref[pl.ds(start, tk), :]
        st = jnp.dot(k, q2_scr[:, gs], preferred_element_type=F32)
        if diag_off is not None and diag_off + tk > qlo + ATTN_CHUNK:
            kk = lax.broadcasted_iota(jnp.int32, st.shape, 0) + diag_off
            qq = lax.broadcasted_iota(jnp.int32, st.shape, 1) + qlo
            st = jnp.where((kk // ATTN_CHUNK) <= (qq // ATTN_CHUNK), st, NEG_BIG)
        s_ref[:, gs] = st
        cm_ref[:, gs] = jnp.max(st, axis=0, keepdims=True)

    def consume_group(start, s_ref, cm_ref, diag_off, g):
        gs = slice(g * QGROUP, (g + 1) * QGROUP)
        if all_masked(diag_off, g):
            return
        vt_ext = jnp.concatenate([vt_ref[:, pl.ds(start, tk)], ones], axis=0)
        m_old = m_scr[:, gs]
        m_new = jnp.maximum(m_old, cm_ref[:, gs])
        alpha = jnp.exp2(m_old - m_new)
        pt = jnp.exp2((s_ref[:, gs] - m_new).astype(BF16))
        acc_scr[:, gs] = alpha * acc_scr[:, gs] + jnp.dot(vt_ext, pt,
                                                          preferred_element_type=F32)
        m_scr[:, gs] = m_new

    def stage(score_args, consume_args):
        for g in range(ng):
            if score_args is not None:
                scores_group(*score_args, g)
            if consume_args is not None:
                consume_group(*consume_args, g)

    lv = lam_ref[...]
    lam = (jnp.exp(jnp.sum(lv[0:1, :] * lv[1:2, :], axis=1, keepdims=True))
           - jnp.exp(jnp.sum(lv[2:3, :] * lv[3:4, :], axis=1, keepdims=True))
           + lambda_init)

    def finalize(i):
        rows = pl.ds(pl.multiple_of(i * tq, tq), tq)
        o0 = acc_scr[0:dv, 0:tq] / acc_scr[dv:dv + 1, 0:tq]
        o1 = acc_scr[0:dv, tq:2 * tq] / acc_scr[dv:dv + 1, tq:2 * tq]
        out_t = o0 - lam * o1
        ms = jnp.mean(out_t * out_t, axis=0, keepdims=True)
        hn = (out_t * lax.rsqrt(ms + EPS)).T
        hn = hn * nw_ref[...] * (1.0 - lambda_init)
        o_ref[rows, :] = (hn * _silu(z_ref[rows, :].astype(F32))).astype(BF16)

    def off(x):
        return pl.multiple_of(x, tk)

    def diag_tail(i):
        d0 = i * tq
        stage((off(d0 + tk), sb, cmb, tk), (off(d0), sa, cma, 0))
        load_queries(jnp.minimum(i + 1, nq - 1))
        stage((0, sa, cma, None), (off(d0 + tk), sb, cmb, tk))
        finalize(i)
        reset_stats()

    reset_stats()
    load_queries(0)
    stage((0, sa, cma, 0), None)
    diag_tail(0)

    def q_block(i, carry):
        def two_steps(t):
            u = 2 * t * tk
            stage((off(u + tk), sb, cmb, None), (off(u), sa, cma, None))
            stage((off(u + 2 * tk), sa, cma, None), (off(u + tk), sb, cmb, None))

        def body(t, c):
            two_steps(2 * t)
            two_steps(2 * t + 1)
            return c

        n_two = i - 1
        lax.fori_loop(0, lax.shift_right_logical(n_two, 1), body, 0)

        @pl.when((n_two & 1) == 1)
        def _():
            two_steps(n_two - 1)

        u = (2 * i - 2) * tk
        stage((off(u + tk), sb, cmb, None), (off(u), sa, cma, None))
        stage((off(i * tq), sa, cma, 0), (off(u + tk), sb, cmb, None))
        diag_tail(i)
        return carry

    lax.fori_loop(1, nq, q_block, 0)


def _attn(lam_vec, qt, kr, vt, p, a_norm_w, *, batch, seq, lambda_init, tq=1024):
    nq = seq // tq
    tk = tq // 2
    z_col0 = 3584 // A_VDIM
    return pl.pallas_call(
        functools.partial(_attn_kernel, tq=tq, tk=tk, nq=nq, lambda_init=lambda_init),
        grid=(batch, A_HEADS),
        in_specs=[
            pl.BlockSpec((4, A_QK_DIM), lambda b, h: (0, 0)),
            pl.BlockSpec((A_VDIM, seq), lambda b, h: (b * A_HEADS + h, 0)),
            pl.BlockSpec((seq, A_VDIM), lambda b, h: (b, h)),
            pl.BlockSpec((A_VDIM, seq), lambda b, h: (b * A_HEADS + h, 0)),
            pl.BlockSpec((seq, A_VDIM), lambda b, h: (b, z_col0 + h)),
            pl.BlockSpec((1, A_VDIM), lambda b, h: (0, 0)),
        ],
        out_specs=pl.BlockSpec((seq, A_VDIM), lambda b, h: (b, h)),
        out_shape=jax.ShapeDtypeStruct((batch * seq, A_WIDTH), BF16),
        scratch_shapes=[
            pltpu.VMEM((1, 2 * tq), F32),
            pltpu.VMEM((A_VDIM + ONES_ROWS, 2 * tq + SCORE_PAD), F32),
            pltpu.VMEM((A_VDIM, 2 * tq), BF16),
            pltpu.VMEM((tk, 2 * tq + SCORE_PAD), F32),
            pltpu.VMEM((tk, 2 * tq + SCORE_PAD), F32),
            pltpu.VMEM((1, 2 * tq), F32),
            pltpu.VMEM((1, 2 * tq), F32),
        ],
        compiler_params=pltpu.CompilerParams(
            dimension_semantics=("arbitrary", "arbitrary"),
            vmem_limit_bytes=VMEM_LIMIT),
        name="diffattn",
    )(lam_vec, qt, kr, vt, p, a_norm_w.reshape(1, A_VDIM))


def _outproj_kernel(ym_ref, ya_ref, w_ref, x_ref, fw_ref, o_ref, *, final):
    yc = jnp.concatenate([ym_ref[...], ya_ref[...]], axis=1)
    xn = x_ref[...] + jnp.dot(yc, w_ref[...], preferred_element_type=F32)
    if final:
        ms = jnp.mean(xn * xn, axis=-1, keepdims=True)
        xn = xn * lax.rsqrt(ms + EPS) * fw_ref[...]
    o_ref[...] = xn


def _outproj(ym, ya, w_out, x2, final_w, *, final, tm=1024):
    rows = x2.shape[0]
    return pl.pallas_call(
        functools.partial(_outproj_kernel, final=final),
        grid=(rows // tm,),
        in_specs=[
            pl.BlockSpec((tm, M_WIDTH), lambda i: (i, 0)),
            pl.BlockSpec((tm, A_WIDTH), lambda i: (i, 0)),
            pl.BlockSpec((D_MODEL, D_MODEL), lambda i: (0, 0)),
            pl.BlockSpec((tm, D_MODEL), lambda i: (i, 0)),
            pl.BlockSpec((1, D_MODEL), lambda i: (0, 0)),
        ],
        out_specs=pl.BlockSpec((tm, D_MODEL), lambda i: (i, 0)),
        out_shape=jax.ShapeDtypeStruct((rows, D_MODEL), F32),
        compiler_params=pltpu.CompilerParams(
            dimension_semantics=("arbitrary",), vmem_limit_bytes=VMEM_LIMIT),
        name="outproj",
    )(ym, ya, w_out, x2, final_w.reshape(1, D_MODEL))


def _gate_lanes(i_part, f_part):
    rows = i_part.shape[0]
    z = lambda n: jnp.zeros((rows, n), i_part.dtype)
    return jnp.concatenate([i_part, z(STAT_ROWS - M_HEADS), f_part,
                            z(GATE_LANES - STAT_ROWS - M_HEADS)], axis=1)


def _rope_tables(seq):
    dh = A_QK_DIM
    inv = 1.0 / (ROPE_THETA ** (jnp.arange(0, dh, 2, dtype=F32) / dh))
    hi = jnp.arange(seq // ROPE_BLOCK, dtype=F32)[:, None] * float(ROPE_BLOCK) * inv[None, :]
    lo = jnp.arange(ROPE_BLOCK, dtype=F32)[:, None] * inv[None, :]
    ch, sh = jnp.cos(hi)[:, None, :], jnp.sin(hi)[:, None, :]
    cl, sl = jnp.cos(lo)[None, :, :], jnp.sin(lo)[None, :, :]
    cos = (ch * cl - sh * sl).reshape(seq, dh // 2)
    sin = (sh * cl + ch * sl).reshape(seq, dh // 2)
    cos = jnp.concatenate([cos] * 4, axis=-1)
    sin = jnp.concatenate([sin] * 4, axis=-1)
    q_scale = (A_QK_DIM ** -0.5) * LOG2E
    hd = 2 * A_QK_DIM
    src = jnp.arange(hd, dtype=jnp.int32)[:, None]
    dst = jnp.arange(hd, dtype=jnp.int32)[None, :]
    first = (dst % dh) < dh // 2
    rot = jnp.where(first & (src == dst + dh // 2), -1.0,
                    jnp.where(~first & (src == dst - dh // 2), 1.0, 0.0)).astype(BF16)
    eye = (src == dst).astype(BF16)
    return cos, sin, (cos * q_scale).T, (sin * q_scale).T, rot, eye


def kernel(x, norm_w, w_in, conv_w, conv_b, i_bias, f_bias, m_norm_w, m_skip,
           lam_q1, lam_k1, lam_q2, lam_k2, a_norm_w, w_out, final_norm_w):
    batch, seq, _ = x.shape
    depth = w_in.shape[0]
    x2 = x.reshape(batch * seq, D_MODEL)
    rope_tables = _rope_tables(seq)
    g0 = 3 * M_WIDTH
    g1 = g0 + 2 * M_HEADS
    for l in range(depth):
        lambda_init = 0.8 - 0.6 * math.exp(-0.3 * l)
        w_main = jnp.concatenate([w_in[l, :, :g0], w_in[l, :, g1:]], axis=1).astype(BF16)
        w_gate = _gate_lanes(w_in[l, :, g0:g0 + M_HEADS], w_in[l, :, g0 + M_HEADS:g1]).astype(BF16)
        gate_bias = _gate_lanes(i_bias[l][None, :], f_bias[l][None, :])
        lam_vec = jnp.stack([lam_q1[l], lam_k1[l], lam_q2[l], lam_k2[l]]).astype(F32)

        p, gates = _inproj(x2, norm_w[l], w_main, w_gate)
        ym = _mlstm(p, gates, conv_w[l], conv_b[l], gate_bias, m_norm_w[l], m_skip[l],
                    batch=batch, seq=seq)
        qt, kr, vt = _rope(p, rope_tables, batch=batch, seq=seq)
        ya = _attn(lam_vec, qt, kr, vt, p, a_norm_w[l], batch=batch, seq=seq,
                   lambda_init=lambda_init)
        x2 = _outproj(ym, ya, w_out[l].astype(BF16), x2, final_norm_w,
                      final=(l == depth - 1))
    return x2.reshape(batch, seq, D_MODEL)
```

```python
import functools
import math

import jax
import jax.numpy as jnp
from jax import lax
from jax.experimental import pallas as pl
from jax.experimental.pallas import tpu as pltpu

F32 = jnp.float32
BF16 = jnp.bfloat16

D_MODEL = 1024
M_WIDTH = 512
M_HEADS = 4
M_HEAD_DIM = 128
CONV_WIDTH = 4
A_WIDTH = 512
A_HEADS = 4
A_VDIM = 128
A_QK_DIM = 64
ATTN_CHUNK = 64
ROPE_THETA = 10000.0
ROPE_BLOCK = 64
EPS = 1e-6
LOG2E = 1.4426950408889634

GATE_LANES = 128
STAT_ROWS = 8

VMEM_LIMIT = 56 * 1024 * 1024
NEG_BIG = -1e30


def _silu(y):
    hy = 0.5 * y
    return hy + hy * jnp.tanh(hy)


def _inproj_kernel(x_ref, nw_ref, wm_ref, wg_ref, wqt_ref, wk_ref, wvt_ref, wz_ref,
                   ck_ref, sk_ref, cq_ref, sq_ref,
                   pm_ref, g_ref, qt_ref, kr_ref, vt_ref, az_ref, h_scr):
    j = pl.program_id(1)

    @pl.when(j == 0)
    def _():
        x = x_ref[...]
        ms = jnp.mean(x * x, axis=-1, keepdims=True)
        h = (x * lax.rsqrt(ms + EPS) * nw_ref[...]).astype(BF16)
        h_scr[...] = h
        g_ref[...] = jnp.dot(h, wg_ref[...], preferred_element_type=F32)
        pm_ref[...] = jnp.dot(h, wm_ref[...], preferred_element_type=F32).astype(BF16)

    @pl.when(j == 1)
    def _():
        h = h_scr[...]
        nt = (((1,), (1,)), ((), ()))
        half = A_QK_DIM // 2

        q_t = lax.dot_general(wqt_ref[...], h, nt, preferred_element_type=F32)
        for m in range(A_WIDTH // A_QK_DIM):
            rows = slice(m * A_QK_DIM, (m + 1) * A_QK_DIM)
            xs = q_t[rows, :]
            swapped = jnp.concatenate([xs[half:, :], xs[:half, :]], axis=0)
            qt_ref[rows, :] = (xs * cq_ref[...] + swapped * sq_ref[...]).astype(BF16)

        k = jnp.dot(h, wk_ref[...], preferred_element_type=F32)
        cos = jnp.concatenate([ck_ref[...]] * A_HEADS, axis=1)
        sin = jnp.concatenate([sk_ref[...]] * A_HEADS, axis=1)
        lane = lax.broadcasted_iota(jnp.int32, k.shape, 1)
        first_half = (lane & (A_QK_DIM - 1)) < half
        fwd = pltpu.roll(k, half, 1)
        bwd = pltpu.roll(k, A_WIDTH - half, 1)
        kr_ref[...] = (k * cos + jnp.where(first_half, bwd, fwd) * sin).astype(BF16)

        vt_ref[...] = lax.dot_general(wvt_ref[...], h, nt,
                                      preferred_element_type=F32).astype(BF16)
        az_ref[...] = jnp.dot(h, wz_ref[...], preferred_element_type=F32).astype(BF16)


def _inproj(x2, norm_w, weights, tables, *, batch, seq, tm=1024):
    w_m, w_gate, w_qt, w_k, w_vt, w_z = weights
    cos_k, sin_k, cos_qt, sin_qt = tables
    rows = x2.shape[0]
    nsb = seq // tm
    whole = lambda a: pl.BlockSpec(a.shape, lambda i, j: (0, 0))
    row_blk = lambda w: pl.BlockSpec((tm, w), lambda i, j: (i, 0))
    tr_blk = pl.BlockSpec((A_WIDTH, tm), lambda i, j: (i // nsb, i % nsb))
    nat = lambda w, dt: jax.ShapeDtypeStruct((rows, w), dt)
    tr = jax.ShapeDtypeStruct((batch * A_WIDTH, seq), BF16)
    return pl.pallas_call(
        _inproj_kernel,
        grid=(rows // tm, 2),
        in_specs=[
            row_blk(D_MODEL),
            pl.BlockSpec((1, D_MODEL), lambda i, j: (0, 0)),
            whole(w_m), whole(w_gate), whole(w_qt), whole(w_k), whole(w_vt), whole(w_z),
            pl.BlockSpec((tm, 2 * A_QK_DIM), lambda i, j: (i % nsb, 0)),
            pl.BlockSpec((tm, 2 * A_QK_DIM), lambda i, j: (i % nsb, 0)),
            pl.BlockSpec((A_QK_DIM, tm), lambda i, j: (0, i % nsb)),
            pl.BlockSpec((A_QK_DIM, tm), lambda i, j: (0, i % nsb)),
        ],
        out_specs=[row_blk(4 * M_WIDTH), row_blk(GATE_LANES), tr_blk, row_blk(A_WIDTH), tr_blk,
                   row_blk(A_WIDTH)],
        out_shape=[nat(4 * M_WIDTH, BF16), nat(GATE_LANES, F32), tr, nat(A_WIDTH, BF16), tr,
                   nat(A_WIDTH, BF16)],
        scratch_shapes=[pltpu.VMEM((tm, D_MODEL), BF16)],
        compiler_params=pltpu.CompilerParams(
            dimension_semantics=("arbitrary", "arbitrary"),
            vmem_limit_bytes=VMEM_LIMIT),
        name="inproj",
    )(x2, norm_w.reshape(1, D_MODEL), w_m, w_gate, w_qt, w_k, w_vt, w_z,
      cos_k, sin_k, cos_qt, sin_qt)


def _dot_f32_exact_rhs01(x, tri):
    hi = x.astype(BF16)
    r1 = x - hi.astype(F32)
    mid = r1.astype(BF16)
    lo = (r1 - mid.astype(F32)).astype(BF16)
    d = lambda t: jnp.dot(t, tri, preferred_element_type=F32)
    return d(hi) + d(mid) + d(lo)


def _mlstm_kernel(qk_ref, v_ref, z_ref, g_ref, cw_ref, cb_ref, gb_ref, nw_ref, sk_ref,
                  o_ref, xbuf, qkc, hm_scr, c_scr, m_scr, *, L, batch):
    n = pl.program_id(0)
    D = M_HEAD_DIM
    R = STAT_ROWS
    bs = range(batch)

    @pl.when(n == 0)
    def _():
        xbuf[:, 0:8, :] = jnp.zeros((batch, 8, 2 * M_WIDTH), F32)
        c_scr[...] = jnp.zeros(c_scr.shape, F32)
        m_scr[...] = jnp.zeros(m_scr.shape, F32)

    row = lax.broadcasted_iota(jnp.int32, (L, L), 0)
    col = lax.broadcasted_iota(jnp.int32, (L, L), 1)
    causal = col <= row
    triu = jnp.where(row <= col, 1.0, 0.0).astype(BF16)
    lane = lax.broadcasted_iota(jnp.int32, (R, L), 1)

    def gate_stats(bi):
        gt = (g_ref[bi] + gb_ref[...]).T
        li = gt[0:R, :]
        fp = gt[R:2 * R, :]
        lf = jnp.minimum(fp, 0.0) - jnp.log1p(jnp.exp(-jnp.abs(fp)))
        b = _dot_f32_exact_rhs01(lf, triu)
        a = li - b
        pm = a
        shift = 1
        while shift < L:
            pm = jnp.maximum(pm, jnp.where(lane >= shift, pltpu.roll(pm, shift, 1), NEG_BIG))
            shift *= 2
        m_old = m_scr[bi]
        m_prev = jnp.concatenate([m_old] * (L // m_old.shape[1]), axis=1)
        m_i = jnp.maximum(b + m_prev, b + pm)
        w_inter = jnp.exp(b + m_prev - m_i)
        e_neg = jnp.exp(-m_i)
        b_last = b[:, L - 1:L]
        m_new = b_last + jnp.maximum(m_old[:, 0:1], pm[:, L - 1:L])
        decay = jnp.exp(b_last + m_old[:, 0:1] - m_new)
        wk = jnp.exp(b_last + a - m_new)
        m_scr[bi] = jnp.broadcast_to(m_new, m_old.shape)
        cols = jnp.concatenate([(m_i - b) * LOG2E, w_inter, e_neg, wk,
                                jnp.zeros((GATE_LANES - 4 * R, L), F32)], axis=0).T
        return a * LOG2E, cols, decay

    stats = []
    for bi in bs:
        stats.append(gate_stats(bi))
        xb = xbuf.at[bi]
        xb[8:8 + L, :] = qk_ref[bi].astype(F32)
        y = cb_ref[...] + cw_ref[3:4, :] * xb[8:8 + L, :]
        y = y + cw_ref[2:3, :] * xb[7:7 + L, :]
        y = y + cw_ref[1:2, :] * xb[6:6 + L, :]
        y = y + cw_ref[0:1, :] * xb[5:5 + L, :]
        xb[0:8, :] = xb[L:L + 8, :]
        qkc[bi] = _silu(y)

    ones = jnp.ones((L, D), BF16)
    for h in range(M_HEADS):
        sl = slice(h * D, (h + 1) * D)
        for bi in bs:
            a2, cols, decay = stats[bi]
            c_col = cols[:, h:h + 1]
            wi_col = cols[:, R + h:R + h + 1]
            en_col = cols[:, 2 * R + h:2 * R + h + 1]
            wk_col = cols[:, 3 * R + h:3 * R + h + 1]
            w_intra = jnp.where(causal, jnp.exp2(a2[h:h + 1, :] - c_col), 0.0)

            qb = qkc[bi, :, sl].astype(BF16)
            kh = qkc[bi, :, M_WIDTH + h * D:M_WIDTH + (h + 1) * D] * (D ** -0.5)
            v_ext = jnp.concatenate([v_ref[bi, :, sl], ones], axis=1)
            s = lax.dot_general(qb, kh.astype(BF16), (((1,), (1,)), ((), ())),
                                preferred_element_type=F32) * w_intra
            c_old = c_scr[bi, h]
            q_c = jnp.dot(qb, c_old.astype(BF16), preferred_element_type=F32)
            s_v = jnp.dot(s.astype(BF16), v_ext, preferred_element_type=F32)
            num = wi_col * q_c[:, :D] + s_v[:, :D]
            den = wi_col * q_c[:, D:D + 1] + s_v[:, D:D + 1]
            hm_scr[bi, :, sl] = num * (1.0 / jnp.maximum(jnp.abs(den), en_col))

            kw_t = (kh * wk_col).T.astype(BF16)
            c_scr[bi, h] = decay[h:h + 1, :] * c_old + jnp.dot(kw_t, v_ext,
                                                               preferred_element_type=F32)

    for h in range(M_HEADS):
        sl = slice(h * D, (h + 1) * D)
        for bi in bs:
            hm = hm_scr[bi, :, sl]
            ms = jnp.mean(hm * hm, axis=-1, keepdims=True)
            hn = hm * lax.rsqrt(ms + EPS) * nw_ref[:, sl]
            ym = (hn + sk_ref[:, sl] * qkc[bi, :, sl]) * _silu(z_ref[bi, :, sl].astype(F32))
            o_ref[bi, :, sl] = ym.astype(BF16)


def _mlstm(p, gates, conv_w, conv_b, gate_bias, m_norm_w, m_skip, *, batch, seq, L=512):
    nchunk = seq // L
    p3 = p.reshape(batch, seq, p.shape[1])
    g3 = gates.reshape(batch, seq, GATE_LANES)
    const = lambda n: (0, 0)
    ym = pl.pallas_call(
        functools.partial(_mlstm_kernel, L=L, batch=batch),
        grid=(nchunk,),
        in_specs=[
            pl.BlockSpec((batch, L, 2 * M_WIDTH), lambda n: (0, n, 0)),
            pl.BlockSpec((batch, L, M_WIDTH), lambda n: (0, n, 2)),
            pl.BlockSpec((batch, L, M_WIDTH), lambda n: (0, n, 3)),
            pl.BlockSpec((batch, L, GATE_LANES), lambda n: (0, n, 0)),
            pl.BlockSpec((CONV_WIDTH, 2 * M_WIDTH), const),
            pl.BlockSpec((1, 2 * M_WIDTH), const),
            pl.BlockSpec((1, GATE_LANES), const),
            pl.BlockSpec((1, M_WIDTH), const),
            pl.BlockSpec((1, M_WIDTH), const),
        ],
        out_specs=pl.BlockSpec((batch, L, M_WIDTH), lambda n: (0, n, 0)),
        out_shape=jax.ShapeDtypeStruct((batch, seq, M_WIDTH), BF16),
        scratch_shapes=[
            pltpu.VMEM((batch, L + 8, 2 * M_WIDTH), F32),
            pltpu.VMEM((batch, L, 2 * M_WIDTH), F32),
            pltpu.VMEM((batch, L, M_WIDTH), F32),
            pltpu.VMEM((batch, M_HEADS, M_HEAD_DIM, 2 * M_HEAD_DIM), F32),
            pltpu.VMEM((batch, STAT_ROWS, 128), F32),
        ],
        compiler_params=pltpu.CompilerParams(
            dimension_semantics=("arbitrary",),
            vmem_limit_bytes=VMEM_LIMIT),
        name="mlstm",
    )(p3, p3, p3, g3, conv_w, conv_b.reshape(1, -1), gate_bias,
      m_norm_w.reshape(1, -1), m_skip.reshape(1, -1))
    return ym.reshape(batch * seq, M_WIDTH)


ONES_ROWS = 16
QGROUP = 256
SCORE_PAD = 128


def _attn_kernel(lam_ref, qt_ref, k_ref, vt_ref, z_ref, nw_ref, o_ref,
                 m_scr, acc_scr, q2_scr, sa, sb, cma, cmb, *, tq, tk, nq, lambda_init):
    dv = A_VDIM
    ng = 2 * tq // QGROUP
    ones = jnp.ones((ONES_ROWS, tk), BF16)

    def load_queries(i):
        qt = qt_ref[:, pl.ds(pl.multiple_of(i * tq, tq), tq)]
        rowi = lax.broadcasted_iota(jnp.int32, qt.shape, 0)
        zero = jnp.zeros_like(qt)
        q2_scr[:, 0:tq] = jnp.where(rowi < A_QK_DIM, qt, zero)
        q2_scr[:, tq:] = jnp.where(rowi >= A_QK_DIM, qt, zero)

    def reset_stats():
        m_scr[...] = jnp.full(m_scr.shape, NEG_BIG, F32)
        acc_scr[...] = jnp.zeros(acc_scr.shape, F32)

    def all_masked(diag_off, g):
        return diag_off is not None and diag_off >= (g * QGROUP) % tq + QGROUP

    def scores_group(start, s_ref, cm_ref, diag_off, g):
        gs = slice(g * QGROUP, (g + 1) * QGROUP)
        qlo = (g * QGROUP) % tq
        if all_masked(diag_off, g):
            return
        k = k_ref[pl.ds(start, tk), :]
        st = jnp.dot(k, q2_scr[:, gs], preferred_element_type=F32)
        if diag_off is not None and diag_off + tk > qlo + ATTN_CHUNK:
            kk = lax.broadcasted_iota(jnp.int32, st.shape, 0) + diag_off
            qq = lax.broadcasted_iota(jnp.int32, st.shape, 1) + qlo
            st = jnp.where((kk // ATTN_CHUNK) <= (qq // ATTN_CHUNK), st, NEG_BIG)
        s_ref[:, gs] = st
        cm_ref[:, gs] = jnp.max(st, axis=0, keepdims=True)

    def consume_group(start, s_ref, cm_ref, diag_off, g):
        gs = slice(g * QGROUP, (g + 1) * QGROUP)
        if all_masked(diag_off, g):
            return
        vt_ext = jnp.concatenate([vt_ref[:, pl.ds(start, tk)], ones], axis=0)
        m_old = m_scr[:, gs]
        m_new = jnp.maximum(m_old, cm_ref[:, gs])
        alpha = jnp.exp2(m_old - m_new)
        pt = jnp.exp2((s_ref[:, gs] - m_new).astype(BF16))
        acc_scr[:, gs] = alpha * acc_scr[:, gs] + jnp.dot(vt_ext, pt,
                                                          preferred_element_type=F32)
        m_scr[:, gs] = m_new

    def stage(score_args, consume_args):
        for g in range(ng):
            if score_args is not None:
                scores_group(*score_args, g)
            if consume_args is not None:
                consume_group(*consume_args, g)

    lv = lam_ref[...]
    lam = (jnp.exp(jnp.sum(lv[0:1, :] * lv[1:2, :], axis=1, keepdims=True))
           - jnp.exp(jnp.sum(lv[2:3, :] * lv[3:4, :], axis=1, keepdims=True))
           + lambda_init)

    def finalize(i):
        rows = pl.ds(pl.multiple_of(i * tq, tq), tq)
        o0 = acc_scr[0:dv, 0:tq] / acc_scr[dv:dv + 1, 0:tq]
        o1 = acc_scr[0:dv, tq:2 * tq] / acc_scr[dv:dv + 1, tq:2 * tq]
        out_t = o0 - lam * o1
        ms = jnp.mean(out_t * out_t, axis=0, keepdims=True)
        hn = (out_t * lax.rsqrt(ms + EPS)).T
        hn = hn * nw_ref[...] * (1.0 - lambda_init)
        o_ref[rows, :] = (hn * _silu(z_ref[rows, :].astype(F32))).astype(BF16)

    def off(x):
        return pl.multiple_of(x, tk)

    def diag_tail(i):
        d0 = i * tq
        stage((off(d0 + tk), sb, cmb, tk), (off(d0), sa, cma, 0))
        load_queries(jnp.minimum(i + 1, nq - 1))
        stage((0, sa, cma, None), (off(d0 + tk), sb, cmb, tk))
        finalize(i)
        reset_stats()

    reset_stats()
    load_queries(0)
    stage((0, sa, cma, 0), None)
    diag_tail(0)

    def q_block(i, carry):
        def two_steps(t):
            u = 2 * t * tk
            stage((off(u + tk), sb, cmb, None), (off(u), sa, cma, None))
            stage((off(u + 2 * tk), sa, cma, None), (off(u + tk), sb, cmb, None))

        def body(t, c):
            two_steps(2 * t)
            two_steps(2 * t + 1)
            return c

        n_two = i - 1
        lax.fori_loop(0, lax.shift_right_logical(n_two, 1), body, 0)

        @pl.when((n_two & 1) == 1)
        def _():
            two_steps(n_two - 1)

        u = (2 * i - 2) * tk
        stage((off(u + tk), sb, cmb, None), (off(u), sa, cma, None))
        stage((off(i * tq), sa, cma, 0), (off(u + tk), sb, cmb, None))
        diag_tail(i)
        return carry

    lax.fori_loop(1, nq, q_block, 0)


def _attn(lam_vec, qt, kr, vt, az, a_norm_w, *, batch, seq, lambda_init, tq=1024):
    nq = seq // tq
    tk = tq // 2
    return pl.pallas_call(
        functools.partial(_attn_kernel, tq=tq, tk=tk, nq=nq, lambda_init=lambda_init),
        grid=(batch, A_HEADS),
        in_specs=[
            pl.BlockSpec((4, A_QK_DIM), lambda b, h: (0, 0)),
            pl.BlockSpec((A_VDIM, seq), lambda b, h: (b * A_HEADS + h, 0)),
            pl.BlockSpec((seq, A_VDIM), lambda b, h: (b, h)),
            pl.BlockSpec((A_VDIM, seq), lambda b, h: (b * A_HEADS + h, 0)),
            pl.BlockSpec((seq, A_VDIM), lambda b, h: (b, h)),
            pl.BlockSpec((1, A_VDIM), lambda b, h: (0, 0)),
        ],
        out_specs=pl.BlockSpec((seq, A_VDIM), lambda b, h: (b, h)),
        out_shape=jax.ShapeDtypeStruct((batch * seq, A_WIDTH), BF16),
        scratch_shapes=[
            pltpu.VMEM((1, 2 * tq), F32),
            pltpu.VMEM((A_VDIM + ONES_ROWS, 2 * tq + SCORE_PAD), F32),
            pltpu.VMEM((A_VDIM, 2 * tq), BF16),
            pltpu.VMEM((tk, 2 * tq + SCORE_PAD), F32),
            pltpu.VMEM((tk, 2 * tq + SCORE_PAD), F32),
            pltpu.VMEM((1, 2 * tq), F32),
            pltpu.VMEM((1, 2 * tq), F32),
        ],
        compiler_params=pltpu.CompilerParams(
            dimension_semantics=("arbitrary", "arbitrary"),
            vmem_limit_bytes=VMEM_LIMIT),
        name="diffattn",
    )(lam_vec, qt, kr, vt, az, a_norm_w.reshape(1, A_VDIM))


def _outproj_kernel(ym_ref, ya_ref, w_ref, x_ref, fw_ref, o_ref, *, final):
    yc = jnp.concatenate([ym_ref[...], ya_ref[...]], axis=1)
    xn = x_ref[...] + jnp.dot(yc, w_ref[...], preferred_element_type=F32)
    if final:
        ms = jnp.mean(xn * xn, axis=-1, keepdims=True)
        xn = xn * lax.rsqrt(ms + EPS) * fw_ref[...]
    o_ref[...] = xn


def _outproj(ym, ya, w_out, x2, final_w, *, final, tm=1024):
    rows = x2.shape[0]
    return pl.pallas_call(
        functools.partial(_outproj_kernel, final=final),
        grid=(rows // tm,),
        in_specs=[
            pl.BlockSpec((tm, M_WIDTH), lambda i: (i, 0)),
            pl.BlockSpec((tm, A_WIDTH), lambda i: (i, 0)),
            pl.BlockSpec((D_MODEL, D_MODEL), lambda i: (0, 0)),
            pl.BlockSpec((tm, D_MODEL), lambda i: (i, 0)),
            pl.BlockSpec((1, D_MODEL), lambda i: (0, 0)),
        ],
        out_specs=pl.BlockSpec((tm, D_MODEL), lambda i: (i, 0)),
        out_shape=jax.ShapeDtypeStruct((rows, D_MODEL), F32),
        compiler_params=pltpu.CompilerParams(
            dimension_semantics=("arbitrary",), vmem_limit_bytes=VMEM_LIMIT),
        name="outproj",
    )(ym, ya, w_out, x2, final_w.reshape(1, D_MODEL))


def _gate_lanes(i_part, f_part):
    rows = i_part.shape[0]
    z = lambda n: jnp.zeros((rows, n), i_part.dtype)
    return jnp.concatenate([i_part, z(STAT_ROWS - M_HEADS), f_part,
                            z(GATE_LANES - STAT_ROWS - M_HEADS)], axis=1)


def _rope_tables(seq):
    dh = A_QK_DIM
    inv = 1.0 / (ROPE_THETA ** (jnp.arange(0, dh, 2, dtype=F32) / dh))
    hi = jnp.arange(seq // ROPE_BLOCK, dtype=F32)[:, None] * float(ROPE_BLOCK) * inv[None, :]
    lo = jnp.arange(ROPE_BLOCK, dtype=F32)[:, None] * inv[None, :]
    ch, sh = jnp.cos(hi)[:, None, :], jnp.sin(hi)[:, None, :]
    cl, sl = jnp.cos(lo)[None, :, :], jnp.sin(lo)[None, :, :]
    cos = (ch * cl - sh * sl).reshape(seq, dh // 2)
    sin = (sh * cl + ch * sl).reshape(seq, dh // 2)
    cos_map = jnp.concatenate([cos, cos], axis=-1)
    sin_map = jnp.concatenate([-sin, sin], axis=-1)
    q_scale = (A_QK_DIM ** -0.5) * LOG2E
    cos_k = jnp.concatenate([cos_map, cos_map], axis=-1)
    sin_k = jnp.concatenate([sin_map, sin_map], axis=-1)
    return cos_k, sin_k, (cos_map * q_scale).T, (sin_map * q_scale).T


def kernel(x, norm_w, w_in, conv_w, conv_b, i_bias, f_bias, m_norm_w, m_skip,
           lam_q1, lam_k1, lam_q2, lam_k2, a_norm_w, w_out, final_norm_w):
    batch, seq, _ = x.shape
    depth = w_in.shape[0]
    x2 = x.reshape(batch * seq, D_MODEL)
    rope_tables = _rope_tables(seq)
    g0 = 3 * M_WIDTH
    g1 = g0 + 2 * M_HEADS
    seg = lambda wl, k: wl[:, g1 + k * M_WIDTH:g1 + (k + 1) * M_WIDTH]
    for l in range(depth):
        lambda_init = 0.8 - 0.6 * math.exp(-0.3 * l)
        wl = w_in[l]
        weights = (
            jnp.concatenate([wl[:, :g0], seg(wl, 0)], axis=1).astype(BF16),
            _gate_lanes(wl[:, g0:g0 + M_HEADS], wl[:, g0 + M_HEADS:g1]).astype(BF16),
            seg(wl, 1).T.astype(BF16),
            seg(wl, 2).astype(BF16),
            seg(wl, 3).T.astype(BF16),
            seg(wl, 4).astype(BF16),
        )
        gate_bias = _gate_lanes(i_bias[l][None, :], f_bias[l][None, :])
        lam_vec = jnp.stack([lam_q1[l], lam_k1[l], lam_q2[l], lam_k2[l]]).astype(F32)

        p, gates, qt, kr, vt, az = _inproj(x2, norm_w[l], weights, rope_tables,
                                           batch=batch, seq=seq)
        ym = _mlstm(p, gates, conv_w[l], conv_b[l], gate_bias, m_norm_w[l], m_skip[l],
                    batch=batch, seq=seq)
        ya = _attn(lam_vec, qt, kr, vt, az, a_norm_w[l], batch=batch, seq=seq,
                   lambda_init=lambda_init)
        x2 = _outproj(ym, ya, w_out[l].astype(BF16), x2, final_norm_w,
                      final=(l == depth - 1))
    return x2.reshape(batch, seq, D_MODEL)
```

```python
import functools
import math

import jax
import jax.numpy as jnp
from jax import lax
from jax.experimental import pallas as pl
from jax.experimental.pallas import tpu as pltpu

F32 = jnp.float32
BF16 = jnp.bfloat16

D_MODEL = 1024
M_WIDTH = 512
M_HEADS = 4
M_HEAD_DIM = 128
CONV_WIDTH = 4
A_WIDTH = 512
A_HEADS = 4
A_VDIM = 128
A_QK_DIM = 64
ATTN_CHUNK = 64
ROPE_THETA = 10000.0
ROPE_BLOCK = 64
EPS = 1e-6
LOG2E = 1.4426950408889634

LANES = 128
SUBLANES = 8
GATE_LANES = LANES
STAT_ROWS = SUBLANES
CONV_HIST = SUBLANES

VMEM_LIMIT = 56 * 1024 * 1024
NEG_BIG = -1e30


def _silu(y):
    hy = 0.5 * y
    return hy + hy * jnp.tanh(hy)


def _inproj_kernel(x_ref, nw_ref, wm_ref, wg_ref, wqt_ref, wk_ref, wvt_ref, wz_ref,
                   ck_ref, sk_ref, cq_ref, sq_ref,
                   pm_ref, g_ref, qt_ref, kr_ref, vt_ref, az_ref, h_scr):
    j = pl.program_id(1)

    @pl.when(j == 0)
    def _():
        x = x_ref[...]
        ms = jnp.mean(x * x, axis=-1, keepdims=True)
        h = (x * lax.rsqrt(ms + EPS) * nw_ref[...]).astype(BF16)
        h_scr[...] = h
        g_ref[...] = jnp.dot(h, wg_ref[...], preferred_element_type=F32)
        pm_ref[...] = jnp.dot(h, wm_ref[...], preferred_element_type=F32).astype(BF16)

    @pl.when(j == 1)
    def _():
        h = h_scr[...]
        nt = (((1,), (1,)), ((), ()))
        half = A_QK_DIM // 2

        q_t = lax.dot_general(wqt_ref[...], h, nt, preferred_element_type=F32)
        for m in range(A_WIDTH // A_QK_DIM):
            rows = slice(m * A_QK_DIM, (m + 1) * A_QK_DIM)
            xs = q_t[rows, :]
            swapped = jnp.concatenate([xs[half:, :], xs[:half, :]], axis=0)
            qt_ref[rows, :] = (xs * cq_ref[...] + swapped * sq_ref[...]).astype(BF16)

        k = jnp.dot(h, wk_ref[...], preferred_element_type=F32)
        cos = jnp.concatenate([ck_ref[...]] * A_HEADS, axis=1)
        sin = jnp.concatenate([sk_ref[...]] * A_HEADS, axis=1)
        lane = lax.broadcasted_iota(jnp.int32, k.shape, 1)
        first_half = (lane & (A_QK_DIM - 1)) < half
        fwd = pltpu.roll(k, half, 1)
        bwd = pltpu.roll(k, A_WIDTH - half, 1)
        kr_ref[...] = (k * cos + jnp.where(first_half, bwd, fwd) * sin).astype(BF16)

        vt_ref[...] = lax.dot_general(wvt_ref[...], h, nt,
                                      preferred_element_type=F32).astype(BF16)
        az_ref[...] = jnp.dot(h, wz_ref[...], preferred_element_type=F32).astype(BF16)


def _inproj(x2, norm_w, weights, tables, *, batch, seq, tm=1024):
    w_m, w_gate, w_qt, w_k, w_vt, w_z = weights
    cos_k, sin_k, cos_qt, sin_qt = tables
    rows = x2.shape[0]
    nsb = seq // tm
    whole = lambda a: pl.BlockSpec(a.shape, lambda i, j: (0, 0))
    row_blk = lambda w: pl.BlockSpec((tm, w), lambda i, j: (i, 0))
    tr_blk = pl.BlockSpec((A_WIDTH, tm), lambda i, j: (i // nsb, i % nsb))
    nat = lambda w, dt: jax.ShapeDtypeStruct((rows, w), dt)
    tr = jax.ShapeDtypeStruct((batch * A_WIDTH, seq), BF16)
    return pl.pallas_call(
        _inproj_kernel,
        grid=(rows // tm, 2),
        in_specs=[
            row_blk(D_MODEL),
            pl.BlockSpec((1, D_MODEL), lambda i, j: (0, 0)),
            whole(w_m), whole(w_gate), whole(w_qt), whole(w_k), whole(w_vt), whole(w_z),
            pl.BlockSpec((tm, 2 * A_QK_DIM), lambda i, j: (i % nsb, 0)),
            pl.BlockSpec((tm, 2 * A_QK_DIM), lambda i, j: (i % nsb, 0)),
            pl.BlockSpec((A_QK_DIM, tm), lambda i, j: (0, i % nsb)),
            pl.BlockSpec((A_QK_DIM, tm), lambda i, j: (0, i % nsb)),
        ],
        out_specs=[row_blk(4 * M_WIDTH), row_blk(GATE_LANES), tr_blk, row_blk(A_WIDTH), tr_blk,
                   row_blk(A_WIDTH)],
        out_shape=[nat(4 * M_WIDTH, BF16), nat(GATE_LANES, F32), tr, nat(A_WIDTH, BF16), tr,
                   nat(A_WIDTH, BF16)],
        scratch_shapes=[pltpu.VMEM((tm, D_MODEL), BF16)],
        compiler_params=pltpu.CompilerParams(
            dimension_semantics=("arbitrary", "arbitrary"),
            vmem_limit_bytes=VMEM_LIMIT),
        name="inproj",
    )(x2, norm_w.reshape(1, D_MODEL), w_m, w_gate, w_qt, w_k, w_vt, w_z,
      cos_k, sin_k, cos_qt, sin_qt)


def _dot_f32_exact_rhs01(x, tri):
    hi = x.astype(BF16)
    r1 = x - hi.astype(F32)
    mid = r1.astype(BF16)
    lo = (r1 - mid.astype(F32)).astype(BF16)
    d = lambda t: jnp.dot(t, tri, preferred_element_type=F32)
    return d(hi) + d(mid) + d(lo)


def _mlstm_kernel(qk_ref, v_ref, z_ref, g_ref, cw_ref, cb_ref, gb_ref, nw_ref, sk_ref,
                  o_ref, xbuf, qkc, hm_scr, c_scr, m_scr, *, L, batch):
    n = pl.program_id(0)
    D = M_HEAD_DIM
    R = STAT_ROWS
    bs = range(batch)

    @pl.when(n == 0)
    def _():
        xbuf[:, 0:CONV_HIST, :] = jnp.zeros((batch, CONV_HIST, 2 * M_WIDTH), F32)
        c_scr[...] = jnp.zeros(c_scr.shape, F32)
        m_scr[...] = jnp.zeros(m_scr.shape, F32)

    row = lax.broadcasted_iota(jnp.int32, (L, L), 0)
    col = lax.broadcasted_iota(jnp.int32, (L, L), 1)
    causal = col <= row
    triu = jnp.where(row <= col, 1.0, 0.0).astype(BF16)
    lane = lax.broadcasted_iota(jnp.int32, (R, L), 1)

    def gate_stats(bi):
        gt = (g_ref[bi] + gb_ref[...]).T
        li = gt[0:R, :]
        fp = gt[R:2 * R, :]
        lf = jnp.minimum(fp, 0.0) - jnp.log1p(jnp.exp(-jnp.abs(fp)))
        b = _dot_f32_exact_rhs01(lf, triu)
        a = li - b
        pm = a
        shift = 1
        while shift < L:
            pm = jnp.maximum(pm, jnp.where(lane >= shift, pltpu.roll(pm, shift, 1), NEG_BIG))
            shift *= 2
        m_old = m_scr[bi]
        m_prev = jnp.concatenate([m_old] * (L // m_old.shape[1]), axis=1)
        m_i = jnp.maximum(b + m_prev, b + pm)
        w_inter = jnp.exp(b + m_prev - m_i)
        e_neg = jnp.exp(-m_i)
        b_last = b[:, L - 1:L]
        m_new = b_last + jnp.maximum(m_old[:, 0:1], pm[:, L - 1:L])
        decay = jnp.exp(b_last + m_old[:, 0:1] - m_new)
        wk = jnp.exp(b_last + a - m_new)
        m_scr[bi] = jnp.broadcast_to(m_new, m_old.shape)
        cols = jnp.concatenate([(m_i - b) * LOG2E, w_inter, e_neg, wk,
                                jnp.zeros((GATE_LANES - 4 * R, L), F32)], axis=0).T
        return a * LOG2E, cols, decay

    stats = []
    for bi in bs:
        stats.append(gate_stats(bi))
        xb = xbuf.at[bi]
        xb[CONV_HIST:CONV_HIST + L, :] = qk_ref[bi].astype(F32)
        y = cb_ref[...]
        for k in reversed(range(CONV_WIDTH)):
            r0 = CONV_HIST - (CONV_WIDTH - 1 - k)
            y = y + cw_ref[k:k + 1, :] * xb[r0:r0 + L, :]
        xb[0:CONV_HIST, :] = xb[L:L + CONV_HIST, :]
        qkc[bi] = _silu(y)

    ones = jnp.ones((L, D), BF16)
    for h in range(M_HEADS):
        sl = slice(h * D, (h + 1) * D)
        for bi in bs:
            a2, cols, decay = stats[bi]
            c_col = cols[:, h:h + 1]
            wi_col = cols[:, R + h:R + h + 1]
            en_col = cols[:, 2 * R + h:2 * R + h + 1]
            wk_col = cols[:, 3 * R + h:3 * R + h + 1]
            w_intra = jnp.where(causal, jnp.exp2(a2[h:h + 1, :] - c_col), 0.0)

            qb = qkc[bi, :, sl].astype(BF16)
            kh = qkc[bi, :, M_WIDTH + h * D:M_WIDTH + (h + 1) * D] * (D ** -0.5)
            v_ext = jnp.concatenate([v_ref[bi, :, sl], ones], axis=1)
            s = lax.dot_general(qb, kh.astype(BF16), (((1,), (1,)), ((), ())),
                                preferred_element_type=F32) * w_intra
            c_old = c_scr[bi, h]
            q_c = jnp.dot(qb, c_old.astype(BF16), preferred_element_type=F32)
            s_v = jnp.dot(s.astype(BF16), v_ext, preferred_element_type=F32)
            num = wi_col * q_c[:, :D] + s_v[:, :D]
            den = wi_col * q_c[:, D:D + 1] + s_v[:, D:D + 1]
            hm_scr[bi, :, sl] = num * (1.0 / jnp.maximum(jnp.abs(den), en_col))

            kw_t = (kh * wk_col).T.astype(BF16)
            c_scr[bi, h] = decay[h:h + 1, :] * c_old + jnp.dot(kw_t, v_ext,
                                                               preferred_element_type=F32)

    for h in range(M_HEADS):
        sl = slice(h * D, (h + 1) * D)
        for bi in bs:
            hm = hm_scr[bi, :, sl]
            ms = jnp.mean(hm * hm, axis=-1, keepdims=True)
            hn = hm * lax.rsqrt(ms + EPS) * nw_ref[:, sl]
            ym = (hn + sk_ref[:, sl] * qkc[bi, :, sl]) * _silu(z_ref[bi, :, sl].astype(F32))
            o_ref[bi, :, sl] = ym.astype(BF16)


def _mlstm(p, gates, conv_w, conv_b, gate_bias, m_norm_w, m_skip, *, batch, seq, L=512):
    nchunk = seq // L
    p3 = p.reshape(batch, seq, p.shape[1])
    g3 = gates.reshape(batch, seq, GATE_LANES)
    const = lambda n: (0, 0)
    ym = pl.pallas_call(
        functools.partial(_mlstm_kernel, L=L, batch=batch),
        grid=(nchunk,),
        in_specs=[
            pl.BlockSpec((batch, L, 2 * M_WIDTH), lambda n: (0, n, 0)),
            pl.BlockSpec((batch, L, M_WIDTH), lambda n: (0, n, 2)),
            pl.BlockSpec((batch, L, M_WIDTH), lambda n: (0, n, 3)),
            pl.BlockSpec((batch, L, GATE_LANES), lambda n: (0, n, 0)),
            pl.BlockSpec((CONV_WIDTH, 2 * M_WIDTH), const),
            pl.BlockSpec((1, 2 * M_WIDTH), const),
            pl.BlockSpec((1, GATE_LANES), const),
            pl.BlockSpec((1, M_WIDTH), const),
            pl.BlockSpec((1, M_WIDTH), const),
        ],
        out_specs=pl.BlockSpec((batch, L, M_WIDTH), lambda n: (0, n, 0)),
        out_shape=jax.ShapeDtypeStruct((batch, seq, M_WIDTH), BF16),
        scratch_shapes=[
            pltpu.VMEM((batch, L + CONV_HIST, 2 * M_WIDTH), F32),
            pltpu.VMEM((batch, L, 2 * M_WIDTH), F32),
            pltpu.VMEM((batch, L, M_WIDTH), F32),
            pltpu.VMEM((batch, M_HEADS, M_HEAD_DIM, 2 * M_HEAD_DIM), F32),
            pltpu.VMEM((batch, STAT_ROWS, LANES), F32),
        ],
        compiler_params=pltpu.CompilerParams(
            dimension_semantics=("arbitrary",),
            vmem_limit_bytes=VMEM_LIMIT),
        name="mlstm",
    )(p3, p3, p3, g3, conv_w, conv_b.reshape(1, -1), gate_bias,
      m_norm_w.reshape(1, -1), m_skip.reshape(1, -1))
    return ym.reshape(batch * seq, M_WIDTH)


ONES_ROWS = 16
QGROUP = 256
SCORE_PAD = LANES


def _attn_kernel(lam_ref, qt_ref, k_ref, vt_ref, z_ref, nw_ref, o_ref,
                 m_scr, acc_scr, q2_scr, sa, sb, cma, cmb, *, tq, tk, nq, lambda_init):
    dv = A_VDIM
    ng = 2 * tq // QGROUP
    ones = jnp.ones((ONES_ROWS, tk), BF16)

    def load_queries(i):
        qt = qt_ref[:, pl.ds(pl.multiple_of(i * tq, tq), tq)]
        rowi = lax.broadcasted_iota(jnp.int32, qt.shape, 0)
        zero = jnp.zeros_like(qt)
        q2_scr[:, 0:tq] = jnp.where(rowi < A_QK_DIM, qt, zero)
        q2_scr[:, tq:] = jnp.where(rowi >= A_QK_DIM, qt, zero)

    def reset_stats():
        m_scr[...] = jnp.full(m_scr.shape, NEG_BIG, F32)
        acc_scr[...] = jnp.zeros(acc_scr.shape, F32)

    def all_masked(diag_off, g):
        return diag_off is not None and diag_off >= (g * QGROUP) % tq + QGROUP

    def scores_group(start, s_ref, cm_ref, diag_off, g):
        gs = slice(g * QGROUP, (g + 1) * QGROUP)
        qlo = (g * QGROUP) % tq
        if all_masked(diag_off, g):
            return
        k = k_ref[pl.ds(start, tk), :]
        st = jnp.dot(k, q2_scr[:, gs], preferred_element_type=F32)
        if diag_off is not None and diag_off + tk > qlo + ATTN_CHUNK:
            kk = lax.broadcasted_iota(jnp.int32, st.shape, 0) + diag_off
            qq = lax.broadcasted_iota(jnp.int32, st.shape, 1) + qlo
            st = jnp.where((kk // ATTN_CHUNK) <= (qq // ATTN_CHUNK), st, NEG_BIG)
        s_ref[:, gs] = st
        cm_ref[:, gs] = jnp.max(st, axis=0, keepdims=True)

    def consume_group(start, s_ref, cm_ref, diag_off, g):
        gs = slice(g * QGROUP, (g + 1) * QGROUP)
        if all_masked(diag_off, g):
            return
        vt_ext = jnp.concatenate([vt_ref[:, pl.ds(start, tk)], ones], axis=0)
        m_old = m_scr[:, gs]
        m_new = jnp.maximum(m_old, cm_ref[:, gs])
        alpha = jnp.exp2(m_old - m_new)
        pt = jnp.exp2((s_ref[:, gs] - m_new).astype(BF16))
        acc_scr[:, gs] = alpha * acc_scr[:, gs] + jnp.dot(vt_ext, pt,
                                                          preferred_element_type=F32)
        m_scr[:, gs] = m_new

    def stage(score_args, consume_args):
        for g in range(ng):
            if score_args is not None:
                scores_group(*score_args, g)
            if consume_args is not None:
                consume_group(*consume_args, g)

    lv = lam_ref[...]
    lam = (jnp.exp(jnp.sum(lv[0:1, :] * lv[1:2, :], axis=1, keepdims=True))
           - jnp.exp(jnp.sum(lv[2:3, :] * lv[3:4, :], axis=1, keepdims=True))
           + lambda_init)

    def finalize(i):
        rows = pl.ds(pl.multiple_of(i * tq, tq), tq)
        o0 = acc_scr[0:dv, 0:tq] / acc_scr[dv:dv + 1, 0:tq]
        o1 = acc_scr[0:dv, tq:2 * tq] / acc_scr[dv:dv + 1, tq:2 * tq]
        out_t = o0 - lam * o1
        ms = jnp.mean(out_t * out_t, axis=0, keepdims=True)
        hn = (out_t * lax.rsqrt(ms + EPS)).T
        hn = hn * nw_ref[...] * (1.0 - lambda_init)
        o_ref[rows, :] = (hn * _silu(z_ref[rows, :].astype(F32))).astype(BF16)

    def off(x):
        return pl.multiple_of(x, tk)

    def diag_tail(i):
        d0 = i * tq
        stage((off(d0 + tk), sb, cmb, tk), (off(d0), sa, cma, 0))
        load_queries(jnp.minimum(i + 1, nq - 1))
        stage((0, sa, cma, None), (off(d0 + tk), sb, cmb, tk))
        finalize(i)
        reset_stats()

    reset_stats()
    load_queries(0)
    stage((0, sa, cma, 0), None)
    diag_tail(0)

    def q_block(i, carry):
        def two_steps(t):
            u = 2 * t * tk
            stage((off(u + tk), sb, cmb, None), (off(u), sa, cma, None))
            stage((off(u + 2 * tk), sa, cma, None), (off(u + tk), sb, cmb, None))

        def body(t, c):
            two_steps(2 * t)
            two_steps(2 * t + 1)
            return c

        n_two = i - 1
        lax.fori_loop(0, lax.shift_right_logical(n_two, 1), body, 0)

        @pl.when((n_two & 1) == 1)
        def _():
            two_steps(n_two - 1)

        u = (2 * i - 2) * tk
        stage((off(u + tk), sb, cmb, None), (off(u), sa, cma, None))
        stage((off(i * tq), sa, cma, 0), (off(u + tk), sb, cmb, None))
        diag_tail(i)
        return carry

    lax.fori_loop(1, nq, q_block, 0)


def _attn(lam_vec, qt, kr, vt, az, a_norm_w, *, batch, seq, lambda_init, tq=1024):
    nq = seq // tq
    tk = tq // 2
    return pl.pallas_call(
        functools.partial(_attn_kernel, tq=tq, tk=tk, nq=nq, lambda_init=lambda_init),
        grid=(batch, A_HEADS),
        in_specs=[
            pl.BlockSpec((4, A_QK_DIM), lambda b, h: (0, 0)),
            pl.BlockSpec((A_VDIM, seq), lambda b, h: (b * A_HEADS + h, 0)),
            pl.BlockSpec((seq, A_VDIM), lambda b, h: (b, h)),
            pl.BlockSpec((A_VDIM, seq), lambda b, h: (b * A_HEADS + h, 0)),
            pl.BlockSpec((seq, A_VDIM), lambda b, h: (b, h)),
            pl.BlockSpec((1, A_VDIM), lambda b, h: (0, 0)),
        ],
        out_specs=pl.BlockSpec((seq, A_VDIM), lambda b, h: (b, h)),
        out_shape=jax.ShapeDtypeStruct((batch * seq, A_WIDTH), BF16),
        scratch_shapes=[
            pltpu.VMEM((1, 2 * tq), F32),
            pltpu.VMEM((A_VDIM + ONES_ROWS, 2 * tq + SCORE_PAD), F32),
            pltpu.VMEM((A_VDIM, 2 * tq), BF16),
            pltpu.VMEM((tk, 2 * tq + SCORE_PAD), F32),
            pltpu.VMEM((tk, 2 * tq + SCORE_PAD), F32),
            pltpu.VMEM((1, 2 * tq), F32),
            pltpu.VMEM((1, 2 * tq), F32),
        ],
        compiler_params=pltpu.CompilerParams(
            dimension_semantics=("arbitrary", "arbitrary"),
            vmem_limit_bytes=VMEM_LIMIT),
        name="diffattn",
    )(lam_vec, qt, kr, vt, az, a_norm_w.reshape(1, A_VDIM))


def _outproj_kernel(ym_ref, ya_ref, w_ref, x_ref, fw_ref, o_ref, *, final):
    yc = jnp.concatenate([ym_ref[...], ya_ref[...]], axis=1)
    xn = x_ref[...] + jnp.dot(yc, w_ref[...], preferred_element_type=F32)
    if final:
        ms = jnp.mean(xn * xn, axis=-1, keepdims=True)
        xn = xn * lax.rsqrt(ms + EPS) * fw_ref[...]
    o_ref[...] = xn


def _outproj(ym, ya, w_out, x2, final_w, *, final, tm=1024):
    rows = x2.shape[0]
    return pl.pallas_call(
        functools.partial(_outproj_kernel, final=final),
        grid=(rows // tm,),
        in_specs=[
            pl.BlockSpec((tm, M_WIDTH), lambda i: (i, 0)),
            pl.BlockSpec((tm, A_WIDTH), lambda i: (i, 0)),
            pl.BlockSpec((D_MODEL, D_MODEL), lambda i: (0, 0)),
            pl.BlockSpec((tm, D_MODEL), lambda i: (i, 0)),
            pl.BlockSpec((1, D_MODEL), lambda i: (0, 0)),
        ],
        out_specs=pl.BlockSpec((tm, D_MODEL), lambda i: (i, 0)),
        out_shape=jax.ShapeDtypeStruct((rows, D_MODEL), F32),
        compiler_params=pltpu.CompilerParams(
            dimension_semantics=("arbitrary",), vmem_limit_bytes=VMEM_LIMIT),
        name="outproj",
    )(ym, ya, w_out, x2, final_w.reshape(1, D_MODEL))


def _gate_lanes(i_part, f_part):
    rows = i_part.shape[0]
    z = lambda n: jnp.zeros((rows, n), i_part.dtype)
    return jnp.concatenate([i_part, z(STAT_ROWS - M_HEADS), f_part,
                            z(GATE_LANES - STAT_ROWS - M_HEADS)], axis=1)


def _rope_tables(seq):
    dh = A_QK_DIM
    inv = 1.0 / (ROPE_THETA ** (jnp.arange(0, dh, 2, dtype=F32) / dh))
    hi = jnp.arange(seq // ROPE_BLOCK, dtype=F32)[:, None] * float(ROPE_BLOCK) * inv[None, :]
    lo = jnp.arange(ROPE_BLOCK, dtype=F32)[:, None] * inv[None, :]
    ch, sh = jnp.cos(hi)[:, None, :], jnp.sin(hi)[:, None, :]
    cl, sl = jnp.cos(lo)[None, :, :], jnp.sin(lo)[None, :, :]
    cos = (ch * cl - sh * sl).reshape(seq, dh // 2)
    sin = (sh * cl + ch * sl).reshape(seq, dh // 2)
    cos_map = jnp.concatenate([cos, cos], axis=-1)
    sin_map = jnp.concatenate([-sin, sin], axis=-1)
    q_scale = (A_QK_DIM ** -0.5) * LOG2E
    cos_k = jnp.concatenate([cos_map, cos_map], axis=-1)
    sin_k = jnp.concatenate([sin_map, sin_map], axis=-1)
    return cos_k, sin_k, (cos_map * q_scale).T, (sin_map * q_scale).T


def kernel(x, norm_w, w_in, conv_w, conv_b, i_bias, f_bias, m_norm_w, m_skip,
           lam_q1, lam_k1, lam_q2, lam_k2, a_norm_w, w_out, final_norm_w):
    batch, seq, _ = x.shape
    depth = w_in.shape[0]
    x2 = x.reshape(batch * seq, D_MODEL)
    rope_tables = _rope_tables(seq)
    g0 = 3 * M_WIDTH
    g1 = g0 + 2 * M_HEADS
    seg = lambda wl, k: wl[:, g1 + k * M_WIDTH:g1 + (k + 1) * M_WIDTH]
    for l in range(depth):
        lambda_init = 0.8 - 0.6 * math.exp(-0.3 * l)
        wl = w_in[l]
        weights = (
            jnp.concatenate([wl[:, :g0], seg(wl, 0)], axis=1).astype(BF16),
            _gate_lanes(wl[:, g0:g0 + M_HEADS], wl[:, g0 + M_HEADS:g1]).astype(BF16),
            seg(wl, 1).T.astype(BF16),
            seg(wl, 2).astype(BF16),
            seg(wl, 3).T.astype(BF16),
            seg(wl, 4).astype(BF16),
        )
        gate_bias = _gate_lanes(i_bias[l][None, :], f_bias[l][None, :])
        lam_vec = jnp.stack([lam_q1[l], lam_k1[l], lam_q2[l], lam_k2[l]]).astype(F32)

        p, gates, qt, kr, vt, az = _inproj(x2, norm_w[l], weights, rope_tables,
                                           batch=batch, seq=seq)
        ym = _mlstm(p, gates, conv_w[l], conv_b[l], gate_bias, m_norm_w[l], m_skip[l],
                    batch=batch, seq=seq)
        ya = _attn(lam_vec, qt, kr, vt, az, a_norm_w[l], batch=batch, seq=seq,
                   lambda_init=lambda_init)
        x2 = _outproj(ym, ya, w_out[l].astype(BF16), x2, final_norm_w,
                      final=(l == depth - 1))
    return x2.reshape(batch, seq, D_MODEL)
```

```python
import functools
import math

import jax
import jax.numpy as jnp
from jax import lax
from jax.experimental import pallas as pl
from jax.experimental.pallas import tpu as pltpu

F32 = jnp.float32
BF16 = jnp.bfloat16

D_MODEL = 1024
M_WIDTH = 512
M_HEADS = 4
M_HEAD_DIM = 128
CONV_WIDTH = 4
A_WIDTH = 512
A_HEADS = 4
A_VDIM = 128
A_QK_DIM = 64
ATTN_CHUNK = 64
ROPE_THETA = 10000.0
ROPE_BLOCK = 64
EPS = 1e-6
LOG2E = 1.4426950408889634

LANES = 128
SUBLANES = 8
GATE_LANES = LANES
STAT_ROWS = SUBLANES
CONV_HIST = SUBLANES

MIB = 1024 * 1024
VMEM_LIMIT = {"inproj": 48 * MIB, "mlstm": 32 * MIB, "diffattn": 42 * MIB, "outproj": 32 * MIB}
NEG_BIG = -1e30


def _silu(y):
    hy = 0.5 * y
    return hy + hy * jnp.tanh(hy)


def _inproj_kernel(x_ref, nw_ref, wm_ref, wg_ref, wqt_ref, wk_ref, wvt_ref, wz_ref,
                   ck_ref, sk_ref, cq_ref, sq_ref,
                   pm_ref, g_ref, qt_ref, kr_ref, vt_ref, az_ref, h_scr):
    j = pl.program_id(1)

    @pl.when(j == 0)
    def _():
        x = x_ref[...]
        ms = jnp.mean(x * x, axis=-1, keepdims=True)
        h = (x * lax.rsqrt(ms + EPS) * nw_ref[...]).astype(BF16)
        h_scr[...] = h
        g_ref[...] = jnp.dot(h, wg_ref[...], preferred_element_type=F32)
        pm_ref[...] = jnp.dot(h, wm_ref[...], preferred_element_type=F32).astype(BF16)

    @pl.when(j == 1)
    def _():
        h = h_scr[...]
        nt = (((1,), (1,)), ((), ()))
        half = A_QK_DIM // 2

        q_t = lax.dot_general(wqt_ref[...], h, nt, preferred_element_type=F32)
        for m in range(A_WIDTH // A_QK_DIM):
            rows = slice(m * A_QK_DIM, (m + 1) * A_QK_DIM)
            xs = q_t[rows, :]
            swapped = jnp.concatenate([xs[half:, :], xs[:half, :]], axis=0)
            qt_ref[rows, :] = (xs * cq_ref[...] + swapped * sq_ref[...]).astype(BF16)

        k = jnp.dot(h, wk_ref[...], preferred_element_type=F32)
        cos = jnp.concatenate([ck_ref[...]] * A_HEADS, axis=1)
        sin = jnp.concatenate([sk_ref[...]] * A_HEADS, axis=1)
        lane = lax.broadcasted_iota(jnp.int32, k.shape, 1)
        first_half = (lane & (A_QK_DIM - 1)) < half
        fwd = pltpu.roll(k, half, 1)
        bwd = pltpu.roll(k, A_WIDTH - half, 1)
        kr_ref[...] = (k * cos + jnp.where(first_half, bwd, fwd) * sin).astype(BF16)

        vt_ref[...] = lax.dot_general(wvt_ref[...], h, nt,
                                      preferred_element_type=F32).astype(BF16)
        az_ref[...] = jnp.dot(h, wz_ref[...], preferred_element_type=F32).astype(BF16)


def _inproj(x2, norm_w, weights, tables, *, batch, seq, tm=1024):
    w_m, w_gate, w_qt, w_k, w_vt, w_z = weights
    cos_k, sin_k, cos_qt, sin_qt = tables
    rows = x2.shape[0]
    nsb = seq // tm
    whole = lambda a: pl.BlockSpec(a.shape, lambda i, j: (0, 0))
    row_blk = lambda w: pl.BlockSpec((tm, w), lambda i, j: (i, 0))
    tr_blk = pl.BlockSpec((A_WIDTH, tm), lambda i, j: (i // nsb, i % nsb))
    nat = lambda w, dt: jax.ShapeDtypeStruct((rows, w), dt)
    tr = jax.ShapeDtypeStruct((batch * A_WIDTH, seq), BF16)
    return pl.pallas_call(
        _inproj_kernel,
        grid=(rows // tm, 2),
        in_specs=[
            row_blk(D_MODEL),
            pl.BlockSpec((1, D_MODEL), lambda i, j: (0, 0)),
            whole(w_m), whole(w_gate), whole(w_qt), whole(w_k), whole(w_vt), whole(w_z),
            pl.BlockSpec((tm, 2 * A_QK_DIM), lambda i, j: (i % nsb, 0)),
            pl.BlockSpec((tm, 2 * A_QK_DIM), lambda i, j: (i % nsb, 0)),
            pl.BlockSpec((A_QK_DIM, tm), lambda i, j: (0, i % nsb)),
            pl.BlockSpec((A_QK_DIM, tm), lambda i, j: (0, i % nsb)),
        ],
        out_specs=[row_blk(4 * M_WIDTH), row_blk(GATE_LANES), tr_blk, row_blk(A_WIDTH), tr_blk,
                   row_blk(A_WIDTH)],
        out_shape=[nat(4 * M_WIDTH, BF16), nat(GATE_LANES, F32), tr, nat(A_WIDTH, BF16), tr,
                   nat(A_WIDTH, BF16)],
        scratch_shapes=[pltpu.VMEM((tm, D_MODEL), BF16)],
        compiler_params=pltpu.CompilerParams(
            dimension_semantics=("arbitrary", "arbitrary"),
            vmem_limit_bytes=VMEM_LIMIT["inproj"]),
        name="inproj",
    )(x2, norm_w.reshape(1, D_MODEL), w_m, w_gate, w_qt, w_k, w_vt, w_z,
      cos_k, sin_k, cos_qt, sin_qt)


def _dot_f32_exact_rhs01(x, tri):
    hi = x.astype(BF16)
    r1 = x - hi.astype(F32)
    mid = r1.astype(BF16)
    lo = (r1 - mid.astype(F32)).astype(BF16)
    d = lambda t: jnp.dot(t, tri, preferred_element_type=F32)
    return d(hi) + d(mid) + d(lo)


def _mlstm_kernel(qk_ref, v_ref, z_ref, g_ref, cw_ref, cb_ref, gb_ref, nw_ref, sk_ref,
                  o_ref, xbuf, qkc, hm_scr, c_scr, m_scr, *, L, batch):
    n = pl.program_id(0)
    D = M_HEAD_DIM
    R = STAT_ROWS
    bs = range(batch)

    @pl.when(n == 0)
    def _():
        xbuf[:, 0:CONV_HIST, :] = jnp.zeros((batch, CONV_HIST, 2 * M_WIDTH), F32)
        c_scr[...] = jnp.zeros(c_scr.shape, F32)
        m_scr[...] = jnp.zeros(m_scr.shape, F32)

    row = lax.broadcasted_iota(jnp.int32, (L, L), 0)
    col = lax.broadcasted_iota(jnp.int32, (L, L), 1)
    causal = col <= row
    triu = jnp.where(row <= col, 1.0, 0.0).astype(BF16)
    lane = lax.broadcasted_iota(jnp.int32, (R, L), 1)

    def gate_stats(bi):
        gt = (g_ref[bi] + gb_ref[...]).T
        li = gt[0:R, :]
        fp = gt[R:2 * R, :]
        lf = jnp.minimum(fp, 0.0) - jnp.log1p(jnp.exp(-jnp.abs(fp)))
        b = _dot_f32_exact_rhs01(lf, triu)
        a = li - b
        pm = a
        shift = 1
        while shift < L:
            pm = jnp.maximum(pm, jnp.where(lane >= shift, pltpu.roll(pm, shift, 1), NEG_BIG))
            shift *= 2
        m_old = m_scr[bi]
        m_prev = jnp.concatenate([m_old] * (L // m_old.shape[1]), axis=1)
        m_i = jnp.maximum(b + m_prev, b + pm)
        w_inter = jnp.exp(b + m_prev - m_i)
        e_neg = jnp.exp(-m_i)
        b_last = b[:, L - 1:L]
        m_new = b_last + jnp.maximum(m_old[:, 0:1], pm[:, L - 1:L])
        decay = jnp.exp(b_last + m_old[:, 0:1] - m_new)
        wk = jnp.exp(b_last + a - m_new)
        m_scr[bi] = jnp.broadcast_to(m_new, m_old.shape)
        cols = jnp.concatenate([(m_i - b) * LOG2E, w_inter, e_neg, wk,
                                jnp.zeros((GATE_LANES - 4 * R, L), F32)], axis=0).T
        return a * LOG2E, cols, decay

    stats = []
    for bi in bs:
        stats.append(gate_stats(bi))
        xb = xbuf.at[bi]
        xb[CONV_HIST:CONV_HIST + L, :] = qk_ref[bi].astype(F32)
        y = cb_ref[...]
        for k in reversed(range(CONV_WIDTH)):
            r0 = CONV_HIST - (CONV_WIDTH - 1 - k)
            y = y + cw_ref[k:k + 1, :] * xb[r0:r0 + L, :]
        xb[0:CONV_HIST, :] = xb[L:L + CONV_HIST, :]
        qkc[bi] = _silu(y)

    ones = jnp.ones((L, D), BF16)
    for h in range(M_HEADS):
        sl = slice(h * D, (h + 1) * D)
        for bi in bs:
            a2, cols, decay = stats[bi]
            c_col = cols[:, h:h + 1]
            wi_col = cols[:, R + h:R + h + 1]
            en_col = cols[:, 2 * R + h:2 * R + h + 1]
            wk_col = cols[:, 3 * R + h:3 * R + h + 1]
            w_intra = jnp.where(causal, jnp.exp2(a2[h:h + 1, :] - c_col), 0.0)

            qb = qkc[bi, :, sl].astype(BF16)
            kh = qkc[bi, :, M_WIDTH + h * D:M_WIDTH + (h + 1) * D] * (D ** -0.5)
            v_ext = jnp.concatenate([v_ref[bi, :, sl], ones], axis=1)
            s = lax.dot_general(qb, kh.astype(BF16), (((1,), (1,)), ((), ())),
                                preferred_element_type=F32) * w_intra
            c_old = c_scr[bi, h]
            q_c = jnp.dot(qb, c_old.astype(BF16), preferred_element_type=F32)
            s_v = jnp.dot(s.astype(BF16), v_ext, preferred_element_type=F32)
            num = wi_col * q_c[:, :D] + s_v[:, :D]
            den = wi_col * q_c[:, D:D + 1] + s_v[:, D:D + 1]
            hm_scr[bi, :, sl] = num * (1.0 / jnp.maximum(jnp.abs(den), en_col))

            kw_t = (kh * wk_col).T.astype(BF16)
            c_scr[bi, h] = decay[h:h + 1, :] * c_old + jnp.dot(kw_t, v_ext,
                                                               preferred_element_type=F32)

    for h in range(M_HEADS):
        sl = slice(h * D, (h + 1) * D)
        for bi in bs:
            hm = hm_scr[bi, :, sl]
            ms = jnp.mean(hm * hm, axis=-1, keepdims=True)
            hn = hm * lax.rsqrt(ms + EPS) * nw_ref[:, sl]
            ym = (hn + sk_ref[:, sl] * qkc[bi, :, sl]) * _silu(z_ref[bi, :, sl].astype(F32))
            o_ref[bi, :, sl] = ym.astype(BF16)


def _mlstm(p, gates, conv_w, conv_b, gate_bias, m_norm_w, m_skip, *, batch, seq, L=512):
    nchunk = seq // L
    p3 = p.reshape(batch, seq, p.shape[1])
    g3 = gates.reshape(batch, seq, GATE_LANES)
    const = lambda n: (0, 0)
    ym = pl.pallas_call(
        functools.partial(_mlstm_kernel, L=L, batch=batch),
        grid=(nchunk,),
        in_specs=[
            pl.BlockSpec((batch, L, 2 * M_WIDTH), lambda n: (0, n, 0)),
            pl.BlockSpec((batch, L, M_WIDTH), lambda n: (0, n, 2)),
            pl.BlockSpec((batch, L, M_WIDTH), lambda n: (0, n, 3)),
            pl.BlockSpec((batch, L, GATE_LANES), lambda n: (0, n, 0)),
            pl.BlockSpec((CONV_WIDTH, 2 * M_WIDTH), const),
            pl.BlockSpec((1, 2 * M_WIDTH), const),
            pl.BlockSpec((1, GATE_LANES), const),
            pl.BlockSpec((1, M_WIDTH), const),
            pl.BlockSpec((1, M_WIDTH), const),
        ],
        out_specs=pl.BlockSpec((batch, L, M_WIDTH), lambda n: (0, n, 0)),
        out_shape=jax.ShapeDtypeStruct((batch, seq, M_WIDTH), BF16),
        scratch_shapes=[
            pltpu.VMEM((batch, L + CONV_HIST, 2 * M_WIDTH), F32),
            pltpu.VMEM((batch, L, 2 * M_WIDTH), F32),
            pltpu.VMEM((batch, L, M_WIDTH), F32),
            pltpu.VMEM((batch, M_HEADS, M_HEAD_DIM, 2 * M_HEAD_DIM), F32),
            pltpu.VMEM((batch, STAT_ROWS, LANES), F32),
        ],
        compiler_params=pltpu.CompilerParams(
            dimension_semantics=("arbitrary",),
            vmem_limit_bytes=VMEM_LIMIT["mlstm"]),
        name="mlstm",
    )(p3, p3, p3, g3, conv_w, conv_b.reshape(1, -1), gate_bias,
      m_norm_w.reshape(1, -1), m_skip.reshape(1, -1))
    return ym.reshape(batch * seq, M_WIDTH)


ONES_ROWS = 16
QGROUP = 256
SCORE_PAD = LANES


def _attn_kernel(lam_ref, qt_ref, k_ref, vt_ref, z_ref, nw_ref, o_ref,
                 m_scr, acc_scr, q2_scr, sa, sb, cma, cmb, *, tq, tk, nq, lambda_init):
    dv = A_VDIM
    ng = 2 * tq // QGROUP
    ones = jnp.ones((ONES_ROWS, tk), BF16)

    def load_queries(i):
        qt = qt_ref[:, pl.ds(pl.multiple_of(i * tq, tq), tq)]
        rowi = lax.broadcasted_iota(jnp.int32, qt.shape, 0)
        zero = jnp.zeros_like(qt)
        q2_scr[:, 0:tq] = jnp.where(rowi < A_QK_DIM, qt, zero)
        q2_scr[:, tq:] = jnp.where(rowi >= A_QK_DIM, qt, zero)

    def reset_stats():
        m_scr[...] = jnp.full(m_scr.shape, NEG_BIG, F32)
        acc_scr[...] = jnp.zeros(acc_scr.shape, F32)

    def all_masked(diag_off, g):
        return diag_off is not None and diag_off >= (g * QGROUP) % tq + QGROUP

    def scores_group(start, s_ref, cm_ref, diag_off, g):
        gs = slice(g * QGROUP, (g + 1) * QGROUP)
        qlo = (g * QGROUP) % tq
        if all_masked(diag_off, g):
            return
        k = k_ref[pl.ds(start, tk), :]
        st = jnp.dot(k, q2_scr[:, gs], preferred_element_type=F32)
        if diag_off is not None and diag_off + tk > qlo + ATTN_CHUNK:
            kk = lax.broadcasted_iota(jnp.int32, st.shape, 0) + diag_off
            qq = lax.broadcasted_iota(jnp.int32, st.shape, 1) + qlo
            st = jnp.where((kk // ATTN_CHUNK) <= (qq // ATTN_CHUNK), st, NEG_BIG)
        s_ref[:, gs] = st
        cm_ref[:, gs] = jnp.max(st, axis=0, keepdims=True)

    def consume_group(start, s_ref, cm_ref, diag_off, g):
        gs = slice(g * QGROUP, (g + 1) * QGROUP)
        if all_masked(diag_off, g):
            return
        vt_ext = jnp.concatenate([vt_ref[:, pl.ds(start, tk)], ones], axis=0)
        m_old = m_scr[:, gs]
        m_new = jnp.maximum(m_old, cm_ref[:, gs])
        alpha = jnp.exp2(m_old - m_new)
        pt = jnp.exp2((s_ref[:, gs] - m_new).astype(BF16))
        acc_scr[:, gs] = alpha * acc_scr[:, gs] + jnp.dot(vt_ext, pt,
                                                          preferred_element_type=F32)
        m_scr[:, gs] = m_new

    def stage(score_args, consume_args):
        for g in range(ng):
            if score_args is not None:
                scores_group(*score_args, g)
            if consume_args is not None:
                consume_group(*consume_args, g)

    lv = lam_ref[...]
    lam = (jnp.exp(jnp.sum(lv[0:1, :] * lv[1:2, :], axis=1, keepdims=True))
           - jnp.exp(jnp.sum(lv[2:3, :] * lv[3:4, :], axis=1, keepdims=True))
           + lambda_init)

    def finalize(i):
        rows = pl.ds(pl.multiple_of(i * tq, tq), tq)
        o0 = acc_scr[0:dv, 0:tq] / acc_scr[dv:dv + 1, 0:tq]
        o1 = acc_scr[0:dv, tq:2 * tq] / acc_scr[dv:dv + 1, tq:2 * tq]
        out_t = o0 - lam * o1
        ms = jnp.mean(out_t * out_t, axis=0, keepdims=True)
        hn = (out_t * lax.rsqrt(ms + EPS)).T
        hn = hn * nw_ref[...] * (1.0 - lambda_init)
        o_ref[rows, :] = (hn * _silu(z_ref[rows, :].astype(F32))).astype(BF16)

    def off(x):
        return pl.multiple_of(x, tk)

    def diag_tail(i):
        d0 = i * tq
        stage((off(d0 + tk), sb, cmb, tk), (off(d0), sa, cma, 0))
        load_queries(jnp.minimum(i + 1, nq - 1))
        stage((0, sa, cma, None), (off(d0 + tk), sb, cmb, tk))
        finalize(i)
        reset_stats()

    reset_stats()
    load_queries(0)
    stage((0, sa, cma, 0), None)
    diag_tail(0)

    def q_block(i, carry):
        def two_steps(t):
            u = 2 * t * tk
            stage((off(u + tk), sb, cmb, None), (off(u), sa, cma, None))
            stage((off(u + 2 * tk), sa, cma, None), (off(u + tk), sb, cmb, None))

        def body(t, c):
            two_steps(2 * t)
            two_steps(2 * t + 1)
            return c

        n_two = i - 1
        lax.fori_loop(0, lax.shift_right_logical(n_two, 1), body, 0)

        @pl.when((n_two & 1) == 1)
        def _():
            two_steps(n_two - 1)

        u = (2 * i - 2) * tk
        stage((off(u + tk), sb, cmb, None), (off(u), sa, cma, None))
        stage((off(i * tq), sa, cma, 0), (off(u + tk), sb, cmb, None))
        diag_tail(i)
        return carry

    lax.fori_loop(1, nq, q_block, 0)


def _attn(lam_vec, qt, kr, vt, az, a_norm_w, *, batch, seq, lambda_init, tq=1024):
    nq = seq // tq
    tk = tq // 2
    return pl.pallas_call(
        functools.partial(_attn_kernel, tq=tq, tk=tk, nq=nq, lambda_init=lambda_init),
        grid=(batch, A_HEADS),
        in_specs=[
            pl.BlockSpec((4, A_QK_DIM), lambda b, h: (0, 0)),
            pl.BlockSpec((A_VDIM, seq), lambda b, h: (b * A_HEADS + h, 0)),
            pl.BlockSpec((seq, A_VDIM), lambda b, h: (b, h)),
            pl.BlockSpec((A_VDIM, seq), lambda b, h: (b * A_HEADS + h, 0)),
            pl.BlockSpec((seq, A_VDIM), lambda b, h: (b, h)),
            pl.BlockSpec((1, A_VDIM), lambda b, h: (0, 0)),
        ],
        out_specs=pl.BlockSpec((seq, A_VDIM), lambda b, h: (b, h)),
        out_shape=jax.ShapeDtypeStruct((batch * seq, A_WIDTH), BF16),
        scratch_shapes=[
            pltpu.VMEM((1, 2 * tq), F32),
            pltpu.VMEM((A_VDIM + ONES_ROWS, 2 * tq + SCORE_PAD), F32),
            pltpu.VMEM((A_VDIM, 2 * tq), BF16),
            pltpu.VMEM((tk, 2 * tq + SCORE_PAD), F32),
            pltpu.VMEM((tk, 2 * tq + SCORE_PAD), F32),
            pltpu.VMEM((1, 2 * tq), F32),
            pltpu.VMEM((1, 2 * tq), F32),
        ],
        compiler_params=pltpu.CompilerParams(
            dimension_semantics=("arbitrary", "arbitrary"),
            vmem_limit_bytes=VMEM_LIMIT["diffattn"]),
        name="diffattn",
    )(lam_vec, qt, kr, vt, az, a_norm_w.reshape(1, A_VDIM))


def _outproj_kernel(ym_ref, ya_ref, w_ref, x_ref, fw_ref, o_ref, *, final):
    yc = jnp.concatenate([ym_ref[...], ya_ref[...]], axis=1)
    xn = x_ref[...] + jnp.dot(yc, w_ref[...], preferred_element_type=F32)
    if final:
        ms = jnp.mean(xn * xn, axis=-1, keepdims=True)
        xn = xn * lax.rsqrt(ms + EPS) * fw_ref[...]
    o_ref[...] = xn


def _outproj(ym, ya, w_out, x2, final_w, *, final, tm=1024):
    rows = x2.shape[0]
    return pl.pallas_call(
        functools.partial(_outproj_kernel, final=final),
        grid=(rows // tm,),
        in_specs=[
            pl.BlockSpec((tm, M_WIDTH), lambda i: (i, 0)),
            pl.BlockSpec((tm, A_WIDTH), lambda i: (i, 0)),
            pl.BlockSpec((D_MODEL, D_MODEL), lambda i: (0, 0)),
            pl.BlockSpec((tm, D_MODEL), lambda i: (i, 0)),
            pl.BlockSpec((1, D_MODEL), lambda i: (0, 0)),
        ],
        out_specs=pl.BlockSpec((tm, D_MODEL), lambda i: (i, 0)),
        out_shape=jax.ShapeDtypeStruct((rows, D_MODEL), F32),
        compiler_params=pltpu.CompilerParams(
            dimension_semantics=("arbitrary",), vmem_limit_bytes=VMEM_LIMIT["outproj"]),
        name="outproj",
    )(ym, ya, w_out, x2, final_w.reshape(1, D_MODEL))


def _gate_lanes(i_part, f_part):
    rows = i_part.shape[0]
    z = lambda n: jnp.zeros((rows, n), i_part.dtype)
    return jnp.concatenate([i_part, z(STAT_ROWS - M_HEADS), f_part,
                            z(GATE_LANES - STAT_ROWS - M_HEADS)], axis=1)


def _rope_tables(seq):
    dh = A_QK_DIM
    inv = 1.0 / (ROPE_THETA ** (jnp.arange(0, dh, 2, dtype=F32) / dh))
    hi = jnp.arange(seq // ROPE_BLOCK, dtype=F32)[:, None] * float(ROPE_BLOCK) * inv[None, :]
    lo = jnp.arange(ROPE_BLOCK, dtype=F32)[:, None] * inv[None, :]
    ch, sh = jnp.cos(hi)[:, None, :], jnp.sin(hi)[:, None, :]
    cl, sl = jnp.cos(lo)[None, :, :], jnp.sin(lo)[None, :, :]
    cos = (ch * cl - sh * sl).reshape(seq, dh // 2)
    sin = (sh * cl + ch * sl).reshape(seq, dh // 2)
    cos_map = jnp.concatenate([cos, cos], axis=-1)
    sin_map = jnp.concatenate([-sin, sin], axis=-1)
    q_scale = (A_QK_DIM ** -0.5) * LOG2E
    cos_k = jnp.concatenate([cos_map, cos_map], axis=-1)
    sin_k = jnp.concatenate([sin_map, sin_map], axis=-1)
    return cos_k, sin_k, (cos_map * q_scale).T, (sin_map * q_scale).T


def kernel(x, norm_w, w_in, conv_w, conv_b, i_bias, f_bias, m_norm_w, m_skip,
           lam_q1, lam_k1, lam_q2, lam_k2, a_norm_w, w_out, final_norm_w):
    batch, seq, _ = x.shape
    depth = w_in.shape[0]
    x2 = x.reshape(batch * seq, D_MODEL)
    rope_tables = _rope_tables(seq)
    g0 = 3 * M_WIDTH
    g1 = g0 + 2 * M_HEADS
    seg = lambda wl, k: wl[:, g1 + k * M_WIDTH:g1 + (k + 1) * M_WIDTH]
    for l in range(depth):
        lambda_init = 0.8 - 0.6 * math.exp(-0.3 * l)
        wl = w_in[l]
        weights = (
            jnp.concatenate([wl[:, :g0], seg(wl, 0)], axis=1).astype(BF16),
            _gate_lanes(wl[:, g0:g0 + M_HEADS], wl[:, g0 + M_HEADS:g1]).astype(BF16),
            seg(wl, 1).T.astype(BF16),
            seg(wl, 2).astype(BF16),
            seg(wl, 3).T.astype(BF16),
            seg(wl, 4).astype(BF16),
        )
        gate_bias = _gate_lanes(i_bias[l][None, :], f_bias[l][None, :])
        lam_vec = jnp.stack([lam_q1[l], lam_k1[l], lam_q2[l], lam_k2[l]]).astype(F32)

        p, gates, qt, kr, vt, az = _inproj(x2, norm_w[l], weights, rope_tables,
                                           batch=batch, seq=seq)
        ym = _mlstm(p, gates, conv_w[l], conv_b[l], gate_bias, m_norm_w[l], m_skip[l],
                    batch=batch, seq=seq)
        ya = _attn(lam_vec, qt, kr, vt, az, a_norm_w[l], batch=batch, seq=seq,
                   lambda_init=lambda_init)
        x2 = _outproj(ym, ya, w_out[l].astype(BF16), x2, final_norm_w,
                      final=(l == depth - 1))
    return x2.reshape(batch, seq, D_MODEL)
```
